```python
import math, functools
import jax, jax.numpy as jnp
from jax import lax
import numpy as np

D_MODEL = 1024
BATCH = 16
SEQ = 256
DEPTH = 4
DEC_BATCH = 4
DEC_SEQ = 1024
PAST_LEN = 512

GRID_W = 64
MIX_WIDTH = D_MODEL
D_CONV = D_MODEL // 4
CONV_WIDTH = 3
SSM_HEAD_DIM = 64
D_SSM = 3 * D_MODEL // 8
SSM_HEADS = D_SSM // SSM_HEAD_DIM
SSM_GROUPS = 2
SSM_STATE = 128
SSM_CONV_WIDTH = 3
CHUNK = 128
N_XBC = D_SSM + 2 * SSM_GROUPS * SSM_STATE
ATT_HEAD_DIM = 64
D_ATT = MIX_WIDTH - D_CONV - D_SSM
ATT_HEADS = D_ATT // ATT_HEAD_DIM
NA_ROWS = 8
NA_COLS = 16
Q_BLOCK = 128
N_PROJ = 3 * D_CONV + 2 * D_SSM + 2 * SSM_GROUPS * SSM_STATE + 2 * SSM_HEADS + 3 * D_ATT
D_FF = ((8 * D_MODEL // 3 + 127) // 128) * 128
N_EXPERTS = 8
TOP_K = 2
D_FF_EXPERT = D_FF // 2
N_DENSE = (DEPTH + 1) // 2
N_MOE = DEPTH // 2
N_MOD = 6
EPS = 1e-6

kernel_name = 'hybrid_conv_ssd_natten_dit_step'


def rmsnorm(x, w):
    xf = x.astype(jnp.float32)
    y = xf * lax.rsqrt(jnp.mean(xf * xf, axis=-1, keepdims=True) + EPS)
    return (y * w.astype(jnp.float32)).astype(x.dtype)


def adaln(c_vec, w_ada, b_ada):
    mod = jax.nn.silu(c_vec) @ w_ada + b_ada
    return [m[:, None, :] for m in jnp.split(mod, N_MOD, axis=-1)]


def modulate(h, shift, scale):
    return h * (1 + scale) + shift


def dwconv(x, w):
    width = w.shape[0]
    return lax.conv_general_dilated(x, w[:, None, :].astype(x.dtype), window_strides=(1,),
                                    padding=[(width // 2, width // 2)],
                                    dimension_numbers=('NWC', 'WIO', 'NWC'),
                                    feature_group_count=x.shape[-1])


def split_proj(p):
    sizes = (D_CONV, D_CONV, D_CONV, D_SSM, D_SSM, SSM_GROUPS * SSM_STATE, SSM_GROUPS * SSM_STATE,
             2 * SSM_HEADS, D_ATT, D_ATT, D_ATT)
    return jnp.split(p, np.cumsum(sizes)[:-1].tolist(), axis=-1)


def segsum(a):
    cs = jnp.cumsum(a, axis=-1)
    seg = cs[..., :, None] - cs[..., None, :]
    t = a.shape[-1]
    return jnp.where(jnp.tril(jnp.ones((t, t), dtype=bool)), seg, -jnp.inf)


def ssd_chunked(x, dt, a, bm, cm, init):
    b, l, h, p = x.shape
    n = bm.shape[-1]
    nc = l // CHUNK
    xc = (x * dt[..., None]).reshape(b, nc, CHUNK, h, p)
    bc = bm.reshape(b, nc, CHUNK, h, n)
    cc = cm.reshape(b, nc, CHUNK, h, n)
    adt = jnp.transpose((dt * a).reshape(b, nc, CHUNK, h), (0, 3, 1, 2))
    a_cs = jnp.cumsum(adt, axis=-1)
    lmat = jnp.exp(segsum(adt))
    y_diag = jnp.einsum('bclhn,bcshn,bhcls,bcshp->bclhp', cc, bc, lmat, xc)
    decay_states = jnp.exp(a_cs[..., -1:] - a_cs)
    states = jnp.einsum('bclhn,bhcl,bclhp->bchpn', bc, decay_states, xc)
    states = jnp.concatenate([init[:, None], states], axis=1)
    decay_chunk = jnp.exp(segsum(jnp.pad(a_cs[..., -1], ((0, 0), (0, 0), (1, 0)))))
    new_states = jnp.einsum('bhzc,bchpn->bzhpn', decay_chunk, states)
    y_off = jnp.einsum('bclhn,bchpn,bhcl->bclhp', cc, new_states[:, :-1], jnp.exp(a_cs))
    return (y_diag + y_off).reshape(b, l, h, p), new_states[:, -1]


def ssm_mixer(s_x, s_z, s_b, s_c, s_dt, conv_w, conv_b, dt_bias, a_log, d_skip, norm_w, init):
    b, l, _ = s_x.shape
    xbc = jax.nn.silu(dwconv(jnp.concatenate([s_x, s_b, s_c], axis=-1), conv_w) + conv_b)
    xs, bs, cs = jnp.split(xbc, [D_SSM, D_SSM + SSM_GROUPS * SSM_STATE], axis=-1)
    rep = SSM_HEADS // SSM_GROUPS
    xs = xs.reshape(b, l, SSM_HEADS, SSM_HEAD_DIM).astype(jnp.float32)
    bs = jnp.repeat(bs.reshape(b, l, SSM_GROUPS, SSM_STATE), rep, axis=2).astype(jnp.float32)
    cs = jnp.repeat(cs.reshape(b, l, SSM_GROUPS, SSM_STATE), rep, axis=2).astype(jnp.float32)
    dt = jax.nn.softplus(s_dt.reshape(b, l, 2, SSM_HEADS).astype(jnp.float32) + dt_bias.astype(jnp.float32))
    a = -jnp.exp(a_log.astype(jnp.float32))
    init = init.astype(jnp.float32)
    flip = lambda t: jnp.flip(t, axis=1)
    y_f, st_f = ssd_chunked(xs, dt[:, :, 0], a[0], bs, cs, init[:, 0])
    y_b, st_b = ssd_chunked(flip(xs), flip(dt[:, :, 1]), a[1], flip(bs), flip(cs), init[:, 1])
    y = y_f + flip(y_b) + d_skip.astype(jnp.float32)[:, None] * xs
    y = y.reshape(b, l, D_SSM) * jax.nn.silu(s_z.astype(jnp.float32))
    yg = y.reshape(b, l, SSM_GROUPS, D_SSM // SSM_GROUPS)
    yg = yg * lax.rsqrt(jnp.mean(yg * yg, axis=-1, keepdims=True) + EPS)
    y = (yg.reshape(b, l, D_SSM) * norm_w.astype(jnp.float32)).astype(s_x.dtype)
    return y, jnp.stack([st_f, st_b], axis=1).astype(s_x.dtype)


def context_attention(q, k, v):
    b, l, hn, hd = q.shape
    scale = hd ** -0.5
    qb = jnp.moveaxis(q.reshape(b, l // Q_BLOCK, Q_BLOCK, hn, hd), 1, 0)

    def block(qi):
        s = jnp.einsum('bqhd,bkhd->bhqk', qi, k).astype(jnp.float32) * scale
        p = jax.nn.softmax(s, axis=-1).astype(v.dtype)
        return jnp.einsum('bhqk,bkhd->bqhd', p, v)

    o = lax.map(block, qb)
    return jnp.moveaxis(o, 0, 1).reshape(b, l, hn, hd)


def na_attention(q, k, v, k_ctx, v_ctx, rpb):
    b, l, hn, hd = q.shape
    rows = l // GRID_W
    wr = min(NA_ROWS, rows)
    ncb = GRID_W // NA_COLS
    kb = 2 * NA_COLS
    n_loc = wr * kb
    row_start = np.clip(np.arange(rows) - wr // 2, 0, rows - wr)
    key_rows = row_start[:, None] + np.arange(wr)[None, :]
    dr_idx = key_rows - np.arange(rows)[:, None] + NA_ROWS - 1
    col_start = np.clip(np.arange(GRID_W) - NA_COLS // 2, 0, GRID_W - NA_COLS)
    qcol = np.arange(ncb)[:, None] * NA_COLS + np.arange(NA_COLS)[None, :]
    band_start = np.clip(col_start[qcol[:, 0]], 0, GRID_W - kb)
    key_cols = band_start[:, None] + np.arange(kb)[None, :]
    kc = key_cols[:, None, :]
    qc = qcol[:, :, None]
    cst = col_start[qcol][:, :, None]
    valid = ((kc >= cst) & (kc < cst + NA_COLS))[:, :, None, :]
    valid = np.broadcast_to(valid, (ncb, NA_COLS, wr, kb)).reshape(ncb, NA_COLS, n_loc)
    dc_idx = np.clip(kc - qc + NA_COLS - 1, 0, 2 * NA_COLS - 2)
    scale = hd ** -0.5
    k_grid = k.reshape(b, rows, GRID_W, hn, hd)
    v_grid = v.reshape(b, rows, GRID_W, hn, hd)
    q_rows = jnp.moveaxis(q.reshape(b, rows, ncb, NA_COLS, hn, hd), 1, 0)

    def row_attend(args):
        q_r, rows_r, dr_r = args

        def band(t):
            t = t[:, rows_r][:, :, key_cols]
            return jnp.transpose(t, (0, 2, 1, 3, 4, 5)).reshape(b, ncb, n_loc, hn, hd)

        k_r = band(k_grid)
        v_r = band(v_grid)
        bias = rpb[:, dr_r[None, None, :, None], dc_idx[:, :, None, :]].reshape(hn, ncb, NA_COLS, n_loc)
        s_loc = jnp.einsum('bjqhd,bjkhd->bhjqk', q_r, k_r).astype(jnp.float32) * scale + bias.astype(jnp.float32)
        s_loc = jnp.where(valid, s_loc, -jnp.inf)
        s_ctx = jnp.einsum('bjqhd,bkhd->bhjqk', q_r, k_ctx).astype(jnp.float32) * scale
        p = jax.nn.softmax(jnp.concatenate([s_loc, s_ctx], axis=-1), axis=-1).astype(v.dtype)
        return (jnp.einsum('bhjqk,bjkhd->bjqhd', p[..., :n_loc], v_r)
                + jnp.einsum('bhjqk,bkhd->bjqhd', p[..., n_loc:], v_ctx))

    o = lax.map(row_attend, (q_rows, jnp.asarray(key_rows, jnp.int32), jnp.asarray(dr_idx, jnp.int32)))
    return jnp.moveaxis(o, 0, 1).reshape(b, l, hn, hd)


def token_mixers(h, w_in, w_out, conv_a_w, ssm_params, ssm_init, attend):
    b, l, _ = h.shape
    a_b, a_c, a_h, s_x, s_z, s_b, s_c, s_dt, q, k, v = split_proj(h @ w_in)
    y_a = a_b * dwconv(a_c * a_h, conv_a_w)
    y_b, states = ssm_mixer(s_x, s_z, s_b, s_c, s_dt, *ssm_params, ssm_init)
    q = q.reshape(b, l, ATT_HEADS, ATT_HEAD_DIM)
    k = k.reshape(b, l, ATT_HEADS, ATT_HEAD_DIM)
    v = v.reshape(b, l, ATT_HEADS, ATT_HEAD_DIM)
    y_c = attend(q, k, v).reshape(b, l, D_ATT)
    out = jnp.concatenate([y_a, y_b, y_c], axis=-1) @ w_out
    return out, k, v, states


def swiglu(x, w1, w3, w2):
    return (jax.nn.silu(x @ w1) * (x @ w3)) @ w2


def moe(h, router_w, router_b, w1, w3, w2):
    shp = h.shape
    t = h.reshape(-1, shp[-1])
    logits = (t @ router_w + router_b).astype(jnp.float32)
    top_v, top_i = lax.top_k(logits, TOP_K)
    top_w = jax.nn.softmax(top_v, axis=-1)
    gates = jnp.sum(jax.nn.one_hot(top_i, N_EXPERTS, dtype=jnp.float32) * top_w[..., None], axis=1).astype(t.dtype)
    out = jnp.zeros_like(t)
    for e in range(N_EXPERTS):
        out = out + gates[:, e:e + 1] * swiglu(t, w1[e], w3[e], w2[e])
    return out.reshape(shp)


def channel_mixer(h, layer, ffn_w1, ffn_w3, ffn_w2, router_w, router_b, moe_w1, moe_w3, moe_w2):
    i = layer // 2
    if layer % 2 == 0:
        return swiglu(h, ffn_w1[i], ffn_w3[i], ffn_w2[i])
    return moe(h, router_w[i], router_b[i], moe_w1[i], moe_w3[i], moe_w2[i])


def setup_inputs(seed: int = 0) -> dict:
    key = jax.random.key(seed)
    ks = jax.random.split(key, 32)

    def nrm(i, shape, s):
        return jax.random.normal(ks[i], shape, jnp.float32) * s

    dt = jnp.exp(jax.random.uniform(ks[17], (DEPTH, 2, SSM_HEADS), jnp.float32, math.log(1e-3), math.log(1e-1)))
    return {
        'x_prompt': nrm(0, (BATCH, SEQ, D_MODEL), 1.0),
        'x_sample': nrm(1, (DEC_BATCH, DEC_SEQ, D_MODEL), 1.0),
        'cache_k': nrm(2, (DEC_BATCH, DEPTH, PAST_LEN, ATT_HEADS, ATT_HEAD_DIM), 1.0),
        'cache_v': nrm(3, (DEC_BATCH, DEPTH, PAST_LEN, ATT_HEADS, ATT_HEAD_DIM), 1.0),
        'state_ssm': nrm(4, (DEC_BATCH, DEPTH, 2, SSM_HEADS, SSM_HEAD_DIM, SSM_STATE), 0.1),
        'c': nrm(5, (DEC_BATCH, D_MODEL), 1.0),
        'c_ctx': nrm(6, (D_MODEL,), 1.0),
        'norm1_w': 1.0 + nrm(7, (DEPTH, D_MODEL), 0.02),
        'norm2_w': 1.0 + nrm(8, (DEPTH, D_MODEL), 0.02),
        'w_ada': nrm(9, (DEPTH, D_MODEL, N_MOD * D_MODEL), 0.5 * D_MODEL ** -0.5),
        'b_ada': nrm(10, (DEPTH, N_MOD * D_MODEL), 0.01),
        'w_in': nrm(11, (DEPTH, D_MODEL, N_PROJ), D_MODEL ** -0.5),
        'w_out': nrm(12, (DEPTH, MIX_WIDTH, D_MODEL), MIX_WIDTH ** -0.5),
        'conv_a_w': nrm(13, (DEPTH, CONV_WIDTH, D_CONV), CONV_WIDTH ** -0.5),
        'ssm_conv_w': nrm(14, (DEPTH, SSM_CONV_WIDTH, N_XBC), SSM_CONV_WIDTH ** -0.5),
        'ssm_conv_b': nrm(15, (DEPTH, N_XBC), 0.01),
        'dt_bias': dt + jnp.log(-jnp.expm1(-dt)),
        'a_log': jnp.log(jax.random.uniform(ks[16], (DEPTH, 2, SSM_HEADS), jnp.float32, 1.0, 16.0)),
        'd_skip': 1.0 + nrm(18, (DEPTH, SSM_HEADS), 0.1),
        'ssm_norm_w': 1.0 + nrm(19, (DEPTH, D_SSM), 0.02),
        'rpb': nrm(20, (DEPTH, ATT_HEADS, 2 * NA_ROWS - 1, 2 * NA_COLS - 1), 0.1),
        'ffn_w1': nrm(21, (N_DENSE, D_MODEL, D_FF), D_MODEL ** -0.5),
        'ffn_w3': nrm(22, (N_DENSE, D_MODEL, D_FF), D_MODEL ** -0.5),
        'ffn_w2': nrm(23, (N_DENSE, D_FF, D_MODEL), D_FF ** -0.5),
        'router_w': nrm(24, (N_MOE, D_MODEL, N_EXPERTS), D_MODEL ** -0.5),
        'router_b': nrm(25, (N_MOE, N_EXPERTS), 0.01),
        'moe_w1': nrm(26, (N_MOE, N_EXPERTS, D_MODEL, D_FF_EXPERT), D_MODEL ** -0.5),
        'moe_w3': nrm(27, (N_MOE, N_EXPERTS, D_MODEL, D_FF_EXPERT), D_MODEL ** -0.5),
        'moe_w2': nrm(28, (N_MOE, N_EXPERTS, D_FF_EXPERT, D_MODEL), D_FF_EXPERT ** -0.5),
        'final_norm_w': 1.0 + nrm(29, (D_MODEL,), 0.02),
    }


def reference(x_prompt, x_sample, cache_k, cache_v, state_ssm, c, c_ctx, norm1_w, norm2_w, w_ada, b_ada,
              w_in, w_out, conv_a_w, ssm_conv_w, ssm_conv_b, dt_bias, a_log, d_skip, ssm_norm_w, rpb,
              ffn_w1, ffn_w3, ffn_w2, router_w, router_b, moe_w1, moe_w3, moe_w2, final_norm_w):
    xp = x_prompt
    xs = x_sample
    ks, vs, sts = [], [], []
    zero_state = jnp.zeros((xp.shape[0], 2, SSM_HEADS, SSM_HEAD_DIM, SSM_STATE), jnp.float32)
    for layer in range(DEPTH):
        ssm_params = (ssm_conv_w[layer], ssm_conv_b[layer], dt_bias[layer], a_log[layer], d_skip[layer],
                      ssm_norm_w[layer])
        sh1, sc1, g1, sh2, sc2, g2 = adaln(c_ctx[None, :], w_ada[layer], b_ada[layer])
        h = modulate(rmsnorm(xp, norm1_w[layer]), sh1, sc1)
        out, k_ctx_new, v_ctx_new, st_ctx = token_mixers(h, w_in[layer], w_out[layer], conv_a_w[layer],
                                                         ssm_params, zero_state, context_attention)
        xp = xp + g1 * out
        h = modulate(rmsnorm(xp, norm2_w[layer]), sh2, sc2)
        xp = xp + g2 * channel_mixer(h, layer, ffn_w1, ffn_w3, ffn_w2, router_w, router_b, moe_w1, moe_w3, moe_w2)
        ks.append(k_ctx_new)
        vs.append(v_ctx_new)
        sts.append(st_ctx)
        sh1, sc1, g1, sh2, sc2, g2 = adaln(c, w_ada[layer], b_ada[layer])
        h = modulate(rmsnorm(xs, norm1_w[layer]), sh1, sc1)
        attend = functools.partial(na_attention, k_ctx=cache_k[:, layer], v_ctx=cache_v[:, layer], rpb=rpb[layer])
        out, _, _, _ = token_mixers(h, w_in[layer], w_out[layer], conv_a_w[layer], ssm_params,
                                    state_ssm[:, layer], attend)
        xs = xs + g1 * out
        h = modulate(rmsnorm(xs, norm2_w[layer]), sh2, sc2)
        xs = xs + g2 * channel_mixer(h, layer, ffn_w1, ffn_w3, ffn_w2, router_w, router_b, moe_w1, moe_w3, moe_w2)
    y_prompt = rmsnorm(xp, final_norm_w)
    y_sample = rmsnorm(xs, final_norm_w)
    new_cache_k = jnp.stack(ks, axis=1)
    new_cache_v = jnp.stack(vs, axis=1)
    new_state_ssm = jnp.stack(sts, axis=1)
    return (y_prompt, y_sample, new_cache_k, new_cache_v, new_state_ssm)
```

```python
import functools

import numpy as np
import jax
import jax.numpy as jnp
from jax import lax
from jax.experimental import pallas as pl
from jax.experimental.pallas import tpu as pltpu

F32 = jnp.float32
BF16 = jnp.bfloat16

D = 1024
BATCH = 16
SEQ = 256
DEPTH = 4
DEC_BATCH = 4
DEC_SEQ = 1024
PAST_LEN = 512
GRID_W = 64
D_CONV = 256
D_SSM = 384
SSM_HEADS = 6
SSM_HEAD_DIM = 64
SSM_STATE = 128
CHUNK = 128
N_BC = 512
D_ATT = 384
ATT_HEADS = 6
ATT_HEAD_DIM = 64
NA_ROWS = 8
NA_COLS = 16
D_FF = 2816
N_EXPERTS = 8
D_FF_EXPERT = 1408
N_MOD = 6
EPS = 1e-6

T_CTX = BATCH * SEQ
T_LAT = DEC_BATCH * DEC_SEQ
T_ALL = T_CTX + T_LAT

W_A = 3 * D_CONV
W_S = 2 * D_SSM + N_BC
W_QKV = 3 * D_ATT
W_DT = 128
N_PROJ_PAD = W_A + W_S + W_QKV + W_DT

LANES = 128
NEG = -1e30

NA_TQ = 128
NA_TILES = DEC_SEQ // NA_TQ
NA_WIN_ROWS = 10
NA_WIN = NA_WIN_ROWS * GRID_W

VMEM_LIMIT = 56 * 1024 * 1024


def _cparams(n_axes):
    return pltpu.CompilerParams(dimension_semantics=("arbitrary",) * n_axes, vmem_limit_bytes=VMEM_LIMIT)


def _silu(x):
    return x * jax.nn.sigmoid(x)


def _mod_row(i, tm):
    n_ctx = T_CTX // tm
    per_lat = DEC_SEQ // tm
    return jnp.where(i < n_ctx, 0, (i - n_ctx) // per_lat + 1)


def _mod_spec(layer, k, tm):
    return pl.BlockSpec((None, None, 1, D), lambda i, *_: (layer, _mod_row(i, tm), 0, k))


ADA_TN = 1536


def _adaln_kernel(c_ref, w_ref, b_ref, o_ref):
    s = _silu(c_ref[...]).astype(BF16)
    o_ref[...] = jnp.dot(s, w_ref[...].astype(BF16), preferred_element_type=F32) + b_ref[...]


def _adaln_all(cvec, w_ada, b_ada):
    n = N_MOD * D
    return pl.pallas_call(
        _adaln_kernel,
        grid=(DEPTH, n // ADA_TN),
        in_specs=[
            pl.BlockSpec((8, D), lambda l, j: (0, 0)),
            pl.BlockSpec((None, D, ADA_TN), lambda l, j: (l, 0, j)),
            pl.BlockSpec((None, 1, ADA_TN), lambda l, j: (l, 0, j)),
        ],
        out_specs=pl.BlockSpec((None, 8, ADA_TN), lambda l, j: (l, 0, j)),
        out_shape=jax.ShapeDtypeStruct((DEPTH, 8, n), F32),
        compiler_params=_cparams(2),
        name="adaln",
    )(cvec, w_ada, b_ada.reshape(DEPTH, 1, n))


TM_IN = 512


def _norm_mod(x, nw, sh, sc):
    var = jnp.mean(x * x, axis=-1, keepdims=True)
    h = x * lax.rsqrt(var + EPS) * nw
    return h * (1.0 + sc) + sh


def _inproj_kernel(x_ref, nw_ref, sh_ref, sc_ref, w_ref, pa_ref, ps_ref, pqkv_ref, pdt_ref):
    hb = _norm_mod(x_ref[...], nw_ref[...], sh_ref[...], sc_ref[...]).astype(BF16)
    o = 0
    pa_ref[...] = jnp.dot(hb, w_ref[:, o:o + W_A], preferred_element_type=F32).astype(BF16)
    o += W_A
    ps_ref[...] = jnp.dot(hb, w_ref[:, o:o + W_S], preferred_element_type=F32).astype(BF16)
    o += W_S
    pqkv_ref[...] = jnp.dot(hb, w_ref[:, o:o + W_QKV], preferred_element_type=F32).astype(BF16)
    o += W_QKV
    pdt_ref[...] = jnp.dot(hb, w_ref[:, o:o + W_DT], preferred_element_type=F32)


def _inproj(x, mod4, norm1_w3, w_in_r, layer):
    tm = TM_IN
    row = lambda w: pl.BlockSpec((tm, w), lambda i: (i, 0))
    return pl.pallas_call(
        _inproj_kernel,
        grid=(T_ALL // tm,),
        in_specs=[
            row(D),
            pl.BlockSpec((None, 1, D), lambda i: (layer, 0, 0)),
            _mod_spec(layer, 0, tm),
            _mod_spec(layer, 1, tm),
            pl.BlockSpec((None, D, N_PROJ_PAD), lambda i: (layer, 0, 0)),
        ],
        out_specs=[row(W_A), row(W_S), row(W_QKV), row(W_DT)],
        out_shape=[
            jax.ShapeDtypeStruct((T_ALL, W_A), BF16),
            jax.ShapeDtypeStruct((T_ALL, W_S), BF16),
            jax.ShapeDtypeStruct((T_ALL, W_QKV), BF16),
            jax.ShapeDtypeStruct((T_ALL, W_DT), F32),
        ],
        compiler_params=_cparams(1),
        name="inproj",
    )(x, norm1_w3, mod4, mod4, w_in_r)


HALO = 16


def _conv3(cur, prev_row, next_row, w):
    n = cur.shape[0]
    rid = lax.broadcasted_iota(jnp.int32, cur.shape, 0)
    dn = jnp.where(rid == 0, prev_row, pltpu.roll(cur, 1, 0))
    up = jnp.where(rid == n - 1, next_row, pltpu.roll(cur, n - 1, 0))
    return w[0:1] * dn + w[1:2] * cur + w[2:3] * up


def _cumsum_rows(x):
    rid = lax.broadcasted_iota(jnp.int32, x.shape, 0)
    s = 1
    while s < x.shape[0]:
        x = x + jnp.where(rid >= s, pltpu.roll(x, s, 0), 0.0)
        s *= 2
    return x


def _mixer_ab_kernel(seq_len, has_init, *refs):
    if has_init:
        (pa_ref, ps_ref, pdt_ref, init_ref, wa_ref, cw_ref, cb_ref, dtb_ref, alog_ref, dsk_ref, nw_ref,
         yab_ref, xs_ref, bc_ref, dt_ref, y_ref, rt_ref) = refs
        st_ref = None
    else:
        (pa_ref, ps_ref, pdt_ref, wa_ref, cw_ref, cb_ref, dtb_ref, alog_ref, dsk_ref, nw_ref,
         yab_ref, st_ref, xs_ref, bc_ref, dt_ref, y_ref, rt_ref) = refs
        init_ref = None
    nc = seq_len // CHUNK

    def halo(ref, r0, c):
        lo = jnp.maximum(r0 - HALO, 0)
        hi = jnp.minimum(r0 + CHUNK, seq_len - HALO)
        prev = ref[pl.ds(pl.multiple_of(lo, HALO), HALO), :].astype(F32)[HALO - 1:HALO]
        nxt = ref[pl.ds(pl.multiple_of(hi, HALO), HALO), :].astype(F32)[0:1]
        prev = prev * (c > 0).astype(F32)
        nxt = nxt * (c < nc - 1).astype(F32)
        return prev, nxt

    def prep(c, carry):
        r0 = pl.multiple_of(c * CHUNK, CHUNK)
        rows = pl.ds(r0, CHUNK)
        pa = pa_ref[rows, :].astype(F32)
        pp, pn = halo(pa_ref, r0, c)
        g = pa[:, D_CONV:2 * D_CONV] * pa[:, 2 * D_CONV:]
        gp = pp[:, D_CONV:2 * D_CONV] * pp[:, 2 * D_CONV:]
        gn = pn[:, D_CONV:2 * D_CONV] * pn[:, 2 * D_CONV:]
        ya = pa[:, :D_CONV] * _conv3(g, gp, gn, wa_ref[...])
        yab_ref[rows, 0:D_CONV] = ya.astype(BF16)

        ps = ps_ref[rows, :].astype(F32)
        sp, sn = halo(ps_ref, r0, c)
        cw = cw_ref[...]
        cb = cb_ref[...]
        xs = _conv3(ps[:, :D_SSM], sp[:, :D_SSM], sn[:, :D_SSM], cw[:, :D_SSM]) + cb[:, :D_SSM]
        xs_ref[rows, :] = _silu(xs)
        o = 2 * D_SSM
        bc = _conv3(ps[:, o:], sp[:, o:], sn[:, o:], cw[:, D_SSM:]) + cb[:, D_SSM:]
        bc_ref[rows, :] = _silu(bc)
        t = pdt_ref[rows, :] + dtb_ref[...]
        dt_ref[rows, :] = jnp.maximum(t, 0.0) + jnp.log1p(jnp.exp(-jnp.abs(t)))
        return carry

    lax.fori_loop(0, nc, prep, 0)

    for d in range(2):
        for j in range(D_SSM // LANES):
            cols = slice(j * LANES, (j + 1) * LANES)
            if has_init:
                rt_ref[d, :, cols] = init_ref[d, cols, :].T
            else:
                rt_ref[d, :, cols] = jnp.zeros((SSM_STATE, LANES), F32)

    a_row = -jnp.exp(alog_ref[...])

    def ssd_chunk(c, d):
        rows = pl.ds(pl.multiple_of(c * CHUNK, CHUNK), CHUNK)
        xs = xs_ref[rows, :]
        bc = bc_ref[rows, :]
        dt = dt_ref[rows, :]
        adt = dt * a_row
        cs = _cumsum_rows(adt)
        total = cs[CHUNK - 1:CHUNK, :]
        e = cs if d == 0 else total - cs + adt
        e_t = e.T
        dec = jnp.exp(total - e)
        ee = jnp.exp(e)
        etot = jnp.exp(total)
        ri = lax.broadcasted_iota(jnp.int32, (CHUNK, CHUNK), 0)
        ci = lax.broadcasted_iota(jnp.int32, (CHUNK, CHUNK), 1)
        valid = (ri >= ci) if d == 0 else (ri <= ci)
        for g in range(2):
            bg = bc[:, g * SSM_STATE:(g + 1) * SSM_STATE]
            cg = bc[:, 2 * SSM_STATE + g * SSM_STATE:2 * SSM_STATE + (g + 1) * SSM_STATE].astype(BF16)
            cb = lax.dot_general(cg, bg.astype(BF16), (((1,), (1,)), ((), ())), preferred_element_type=F32)
            bg_t = bg.T.astype(BF16)
            for hh in range(SSM_HEADS // 2):
                h = g * (SSM_HEADS // 2) + hh
                k = d * SSM_HEADS + h
                hc = slice(h * SSM_HEAD_DIM, (h + 1) * SSM_HEAD_DIM)
                xdt = xs[:, hc] * dt[:, k:k + 1]
                seg = e[:, k:k + 1] - e_t[k:k + 1, :]
                lm = jnp.where(valid, jnp.exp(jnp.where(valid, seg, 0.0)), 0.0)
                y_diag = jnp.dot((cb * lm).astype(BF16), xdt.astype(BF16), preferred_element_type=F32)
                rt_h = rt_ref[d, :, hc]
                y_off = jnp.dot(cg, rt_h.astype(BF16), preferred_element_type=F32) * ee[:, k:k + 1]
                if d == 0:
                    y_ref[rows, hc] = y_diag + y_off
                else:
                    y_ref[rows, hc] = y_ref[rows, hc] + y_diag + y_off
                st = jnp.dot(bg_t, (xdt * dec[:, k:k + 1]).astype(BF16), preferred_element_type=F32)
                rt_ref[d, :, hc] = rt_h * etot[:, k:k + 1] + st

    lax.fori_loop(0, nc, lambda i, cr: (ssd_chunk(i, 0), cr)[1], 0)
    lax.fori_loop(0, nc, lambda i, cr: (ssd_chunk(nc - 1 - i, 1), cr)[1], 0)

    if not has_init:
        for d in range(2):
            for j in range(D_SSM // LANES):
                cols = slice(j * LANES, (j + 1) * LANES)
                st_ref[d, cols, :] = rt_ref[d, :, cols].T

    def finish(c, carry):
        rows = pl.ds(pl.multiple_of(c * CHUNK, CHUNK), CHUNK)
        y = y_ref[rows, :] + dsk_ref[...] * xs_ref[rows, :]
        y = y * _silu(ps_ref[rows, D_SSM:2 * D_SSM].astype(F32))
        col = lax.broadcasted_iota(jnp.int32, y.shape, 1)
        first = col < D_SSM // 2
        ysq = y * y
        s0 = jnp.sum(jnp.where(first, ysq, 0.0), axis=-1, keepdims=True)
        s1 = jnp.sum(jnp.where(first, 0.0, ysq), axis=-1, keepdims=True)
        inv = 1.0 / (D_SSM // 2)
        r = jnp.where(first, lax.rsqrt(s0 * inv + EPS), lax.rsqrt(s1 * inv + EPS))
        yab_ref[rows, D_CONV:] = (y * r * nw_ref[...]).astype(BF16)
        return carry

    lax.fori_loop(0, nc, finish, 0)


def _mixer_ab(pa, ps, pdt, init, params, seq_len, n_seq, blk0):
    has_init = init is not None
    seq = lambda w: pl.BlockSpec((seq_len, w), lambda s: (blk0 + s, 0))
    full = lambda a: pl.BlockSpec(a.shape, lambda s: (0,) * a.ndim)
    in_specs = [seq(W_A), seq(W_S), seq(W_DT)]
    args = [pa, ps, pdt]
    if has_init:
        in_specs.append(pl.BlockSpec((None, 2, D_SSM, SSM_STATE), lambda s: (s, 0, 0, 0)))
        args.append(init)
    in_specs += [full(p) for p in params]
    args += list(params)
    yab_spec = pl.BlockSpec((seq_len, D_CONV + D_SSM), lambda s: (s, 0))
    yab_shape = jax.ShapeDtypeStruct((n_seq * seq_len, D_CONV + D_SSM), BF16)
    if has_init:
        out_specs, out_shape = yab_spec, yab_shape
    else:
        out_specs = [yab_spec, pl.BlockSpec((None, 2, D_SSM, SSM_STATE), lambda s: (s, 0, 0, 0))]
        out_shape = [yab_shape, jax.ShapeDtypeStruct((n_seq, 2, D_SSM, SSM_STATE), F32)]
    return pl.pallas_call(
        functools.partial(_mixer_ab_kernel, seq_len, has_init),
        grid=(n_seq,),
        in_specs=in_specs,
        out_specs=out_specs,
        out_shape=out_shape,
        scratch_shapes=[
            pltpu.VMEM((seq_len, D_SSM), F32),
            pltpu.VMEM((seq_len, N_BC), F32),
            pltpu.VMEM((seq_len, LANES), F32),
            pltpu.VMEM((seq_len, D_SSM), F32),
            pltpu.VMEM((2, SSM_STATE, D_SSM), F32),
        ],
        compiler_params=_cparams(1),
        name="mixer_ab_lat" if has_init else "mixer_ab_ctx",
    )(*args)


ATT_SCALE = ATT_HEAD_DIM ** -0.5


def _dot_nt(a, b):
    return lax.dot_general(a, b, (((1,), (1,)), ((), ())), preferred_element_type=F32)


def _attn_ctx_kernel(qkv_ref, o_ref):
    for h in range(ATT_HEADS):
        hc = slice(h * ATT_HEAD_DIM, (h + 1) * ATT_HEAD_DIM)
        q = qkv_ref[:, h * ATT_HEAD_DIM:(h + 1) * ATT_HEAD_DIM]
        k = qkv_ref[:, D_ATT + h * ATT_HEAD_DIM:D_ATT + (h + 1) * ATT_HEAD_DIM]
        v = qkv_ref[:, 2 * D_ATT + h * ATT_HEAD_DIM:2 * D_ATT + (h + 1) * ATT_HEAD_DIM]
        s = _dot_nt(q, k) * ATT_SCALE
        p = jnp.exp(s - jnp.max(s, axis=-1, keepdims=True))
        l = jnp.sum(p, axis=-1, keepdims=True)
        o = jnp.dot(p.astype(BF16), v, preferred_element_type=F32)
        o_ref[:, hc] = (o / l).astype(BF16)


def _attn_ctx(pqkv):
    return pl.pallas_call(
        _attn_ctx_kernel,
        grid=(BATCH,),
        in_specs=[pl.BlockSpec((SEQ, W_QKV), lambda s: (s, 0))],
        out_specs=pl.BlockSpec((SEQ, D_ATT), lambda s: (s, 0)),
        out_shape=jax.ShapeDtypeStruct((T_CTX, D_ATT), BF16),
        compiler_params=_cparams(1),
        name="attn_ctx",
    )(pqkv)


def _na_win_start(i):
    return jnp.clip(2 * i - NA_ROWS // 2, 0, DEC_SEQ // GRID_W - NA_WIN_ROWS)


def _attn_na_kernel(qkv_ref, kc_ref, vc_ref, bias_ref, o_ref):
    i = pl.program_id(0)
    q0 = pl.multiple_of(i * NA_TQ, NA_TQ)
    w0 = pl.multiple_of(_na_win_start(i) * GRID_W, 2 * GRID_W)
    for h in range(ATT_HEADS):
        hc = slice(h * ATT_HEAD_DIM, (h + 1) * ATT_HEAD_DIM)
        kcs = slice(D_ATT + h * ATT_HEAD_DIM, D_ATT + (h + 1) * ATT_HEAD_DIM)
        vcs = slice(2 * D_ATT + h * ATT_HEAD_DIM, 2 * D_ATT + (h + 1) * ATT_HEAD_DIM)
        q = qkv_ref[pl.ds(q0, NA_TQ), hc]
        k_loc = qkv_ref[pl.ds(w0, NA_WIN), kcs]
        v_loc = qkv_ref[pl.ds(w0, NA_WIN), vcs]
        k_ctx = kc_ref[:, hc].astype(BF16)
        v_ctx = vc_ref[:, hc].astype(BF16)
        s_loc = _dot_nt(q, k_loc) * ATT_SCALE + bias_ref[h]
        s_ctx = _dot_nt(q, k_ctx) * ATT_SCALE
        m = jnp.maximum(jnp.max(s_loc, axis=-1, keepdims=True), jnp.max(s_ctx, axis=-1, keepdims=True))
        p_loc = jnp.exp(s_loc - m)
        p_ctx = jnp.exp(s_ctx - m)
        l = jnp.sum(p_loc, axis=-1, keepdims=True) + jnp.sum(p_ctx, axis=-1, keepdims=True)
        o = (jnp.dot(p_loc.astype(BF16), v_loc, preferred_element_type=F32)
             + jnp.dot(p_ctx.astype(BF16), v_ctx, preferred_element_type=F32))
        o_ref[:, hc] = (o / l).astype(BF16)


def _attn_na(pqkv, cache_k4, cache_v4, na_bias, layer):
    lat0 = T_CTX // DEC_SEQ
    return pl.pallas_call(
        _attn_na_kernel,
        grid=(NA_TILES, DEC_BATCH),
        in_specs=[
            pl.BlockSpec((DEC_SEQ, W_QKV), lambda i, b: (lat0 + b, 0)),
            pl.BlockSpec((None, None, PAST_LEN, D_ATT), lambda i, b: (b, layer, 0, 0)),
            pl.BlockSpec((None, None, PAST_LEN, D_ATT), lambda i, b: (b, layer, 0, 0)),
            pl.BlockSpec((ATT_HEADS, None, NA_TQ, NA_WIN), lambda i, b: (0, i, 0, 0)),
        ],
        out_specs=pl.BlockSpec((NA_TQ, D_ATT), lambda i, b: (b * NA_TILES + i, 0)),
        out_shape=jax.ShapeDtypeStruct((T_LAT, D_ATT), BF16),
        compiler_params=_cparams(2),
        name="attn_na",
    )(pqkv, cache_k4, cache_v4, na_bias)


def _na_bias_indices():
    rows = DEC_SEQ // GRID_W
    i = np.arange(NA_TILES)[:, None, None]
    q = i * NA_TQ + np.arange(NA_TQ)[None, :, None]
    qr, qc = q // GRID_W, q % GRID_W
    w0 = np.clip(2 * i - NA_ROWS // 2, 0, rows - NA_WIN_ROWS)
    kl = np.arange(NA_WIN)[None, None, :]
    kr, kc = w0 + kl // GRID_W, kl % GRID_W
    rs = np.clip(qr - NA_ROWS // 2, 0, rows - NA_ROWS)
    cs = np.clip(qc - NA_COLS // 2, 0, GRID_W - NA_COLS)
    valid = (kr >= rs) & (kr < rs + NA_ROWS) & (kc >= cs) & (kc < cs + NA_COLS)
    dr = np.clip(kr - qr + NA_ROWS - 1, 0, 2 * NA_ROWS - 2)
    dc = np.clip(kc - qc + NA_COLS - 1, 0, 2 * NA_COLS - 2)
    shape = (NA_TILES, NA_TQ, NA_WIN)
    return (np.broadcast_to(dr, shape).astype(np.int32), np.broadcast_to(dc, shape).astype(np.int32),
            np.broadcast_to(valid, shape))


TM_OUT = 512


def _outproj_kernel(with_router, *refs):
    if with_router:
        (yab_ref, yc_ref, x_ref, w_ref, g1_ref, nw_ref, sh_ref, sc_ref, rw_ref, rb_ref,
         x1_ref, h2_ref, gates_ref) = refs
    else:
        yab_ref, yc_ref, x_ref, w_ref, g1_ref, nw_ref, sh_ref, sc_ref, x1_ref, h2_ref = refs
    n_ab = D_CONV + D_SSM
    out = (jnp.dot(yab_ref[...], w_ref[0:n_ab, :], preferred_element_type=F32)
           + jnp.dot(yc_ref[...], w_ref[n_ab:, :], preferred_element_type=F32))
    x1 = x_ref[...] + g1_ref[...] * out
    x1_ref[...] = x1
    h2 = _norm_mod(x1, nw_ref[...], sh_ref[...], sc_ref[...])
    h2_ref[...] = h2.astype(BF16)
    if with_router:
        logits = jnp.dot(h2, rw_ref[...], preferred_element_type=F32, precision=lax.Precision.HIGHEST)
        logits = logits + rb_ref[...]
        lane = lax.broadcasted_iota(jnp.int32, logits.shape, 1)
        m1 = jnp.max(logits, axis=-1, keepdims=True)
        i1 = jnp.min(jnp.where(logits == m1, lane, LANES), axis=-1, keepdims=True)
        rest = jnp.where(lane == i1, -jnp.inf, logits)
        m2 = jnp.max(rest, axis=-1, keepdims=True)
        i2 = jnp.min(jnp.where(rest == m2, lane, LANES), axis=-1, keepdims=True)
        e2 = jnp.exp(m2 - m1)
        den = 1.0 + e2
        gates_ref[...] = jnp.where(lane == i1, 1.0 / den, 0.0) + jnp.where(lane == i2, e2 / den, 0.0)


def _outproj(yab, yc, x, mod4, w_out_b, norm2_w3, router, layer):
    tm = TM_OUT
    with_router = router is not None
    row = lambda w: pl.BlockSpec((tm, w), lambda i: (i, 0))
    in_specs = [
        row(D_CONV + D_SSM), row(D_ATT), row(D),
        pl.BlockSpec((None, D, D), lambda i: (layer, 0, 0)),
        _mod_spec(layer, 2, tm),
        pl.BlockSpec((None, 1, D), lambda i: (layer, 0, 0)),
        _mod_spec(layer, 3, tm),
        _mod_spec(layer, 4, tm),
    ]
    args = [yab, yc, x, w_out_b, mod4, norm2_w3, mod4, mod4]
    out_specs = [row(D), row(D)]
    out_shape = [jax.ShapeDtypeStruct((T_ALL, D), F32), jax.ShapeDtypeStruct((T_ALL, D), BF16)]
    if with_router:
        rw, rb = router
        in_specs += [pl.BlockSpec(rw.shape, lambda i: (0, 0)), pl.BlockSpec(rb.shape, lambda i: (0, 0))]
        args += [rw, rb]
        out_specs.append(row(LANES))
        out_shape.append(jax.ShapeDtypeStruct((T_ALL, LANES), F32))
    return pl.pallas_call(
        functools.partial(_outproj_kernel, with_router),
        grid=(T_ALL // tm,),
        in_specs=in_specs,
        out_specs=out_specs,
        out_shape=out_shape,
        compiler_params=_cparams(1),
        name="outproj_router" if with_router else "outproj",
    )(*args)


TM_FF = 512


def _ffn_kernel(gated, *refs):
    if gated:
        h_ref, w1_ref, w3_ref, w2_ref, x_ref, g2_ref, gates_ref, o_ref, acc_ref = refs
    else:
        h_ref, w1_ref, w3_ref, w2_ref, x_ref, g2_ref, o_ref, acc_ref = refs
    e = pl.program_id(1)

    @pl.when(e == 0)
    def _():
        acc_ref[...] = jnp.zeros_like(acc_ref)

    h = h_ref[...]
    u = jnp.dot(h, w1_ref[...], preferred_element_type=F32)
    v = jnp.dot(h, w3_ref[...], preferred_element_type=F32)
    y = jnp.dot((_silu(u) * v).astype(BF16), w2_ref[...], preferred_element_type=F32)
    if gated:
        gates = gates_ref[...]
        lane = lax.broadcasted_iota(jnp.int32, gates.shape, 1)
        y = y * jnp.sum(jnp.where(lane == e, gates, 0.0), axis=-1, keepdims=True)
    acc_ref[...] += y

    @pl.when(e == pl.num_programs(1) - 1)
    def _():
        o_ref[...] = x_ref[...] + g2_ref[...] * acc_ref[...]


def _ffn(h2, x1, mod4, w1, w3, w2, gates, layer):
    tm = TM_FF
    gated = gates is not None
    n_chunks = w1.shape[0]
    row = lambda w: pl.BlockSpec((tm, w), lambda i, e: (i, 0))
    in_specs = [
        row(D),
        pl.BlockSpec((None, D, D_FF_EXPERT), lambda i, e: (e, 0, 0)),
        pl.BlockSpec((None, D, D_FF_EXPERT), lambda i, e: (e, 0, 0)),
        pl.BlockSpec((None, D_FF_EXPERT, D), lambda i, e: (e, 0, 0)),
        row(D),
        _mod_spec(layer, 5, tm),
    ]
    args = [h2, w1, w3, w2, x1, mod4]
    if gated:
        in_specs.append(row(LANES))
        args.append(gates)
    return pl.pallas_call(
        functools.partial(_ffn_kernel, gated),
        grid=(T_ALL // tm, n_chunks),
        in_specs=in_specs,
        out_specs=row(D),
        out_shape=jax.ShapeDtypeStruct((T_ALL, D), F32),
        scratch_shapes=[pltpu.VMEM((tm, D), F32)],
        compiler_params=_cparams(2),
        name="moe" if gated else "ffn",
    )(*args)


TM_NORM = 1024


def _final_norm_kernel(x_ref, w_ref, o_ref):
    x = x_ref[...]
    var = jnp.mean(x * x, axis=-1, keepdims=True)
    o_ref[...] = x * lax.rsqrt(var + EPS) * w_ref[...]


def _final_norm(x, w):
    return pl.pallas_call(
        _final_norm_kernel,
        grid=(T_ALL // TM_NORM,),
        in_specs=[pl.BlockSpec((TM_NORM, D), lambda i: (i, 0)), pl.BlockSpec((1, D), lambda i: (0, 0))],
        out_specs=pl.BlockSpec((TM_NORM, D), lambda i: (i, 0)),
        out_shape=jax.ShapeDtypeStruct((T_ALL, D), F32),
        compiler_params=_cparams(1),
        name="final_norm",
    )(x, w.reshape(1, D))


def _pad_lanes(a, fill=0.0):
    pad = [(0, 0)] * (a.ndim - 1) + [(0, LANES - a.shape[-1])]
    return jnp.pad(a, pad, constant_values=fill)


def _reorder_w_in(w_in):
    a_end = W_A
    s_end = a_end + W_S
    dt_end = s_end + 2 * SSM_HEADS
    w_dt = jnp.pad(w_in[:, :, s_end:dt_end], ((0, 0), (0, 0), (0, W_DT - 2 * SSM_HEADS)))
    return jnp.concatenate([w_in[:, :, :s_end], w_in[:, :, dt_end:], w_dt], axis=-1).astype(BF16)


def kernel(x_prompt, x_sample, cache_k, cache_v, state_ssm, c, c_ctx, norm1_w, norm2_w, w_ada, b_ada, w_in, w_out, conv_a_w, ssm_conv_w, ssm_conv_b, dt_bias, a_log, d_skip, ssm_norm_w, rpb, ffn_w1, ffn_w3, ffn_w2, router_w, router_b, moe_w1, moe_w3, moe_w2, final_norm_w):
    x = jnp.concatenate([x_prompt.reshape(T_CTX, D), x_sample.reshape(T_LAT, D)], axis=0)

    cvec = jnp.concatenate([c_ctx[None, :], c, jnp.zeros((8 - 1 - DEC_BATCH, D), F32)], axis=0)
    mod4 = _adaln_all(cvec, w_ada, b_ada).reshape(DEPTH, 8, 1, N_MOD * D)

    w_in_r = _reorder_w_in(w_in)
    w_out_b = w_out.astype(BF16)
    norm1_w3 = norm1_w.reshape(DEPTH, 1, D)
    norm2_w3 = norm2_w.reshape(DEPTH, 1, D)
    n_dense = ffn_w1.shape[0]
    split_cols = lambda w: jnp.transpose(w.reshape(n_dense, D, 2, D_FF_EXPERT), (0, 2, 1, 3)).astype(BF16)
    ffn_w1c, ffn_w3c = split_cols(ffn_w1), split_cols(ffn_w3)
    ffn_w2c = ffn_w2.reshape(n_dense, 2, D_FF_EXPERT, D).astype(BF16)
    moe_w1b, moe_w3b, moe_w2b = moe_w1.astype(BF16), moe_w3.astype(BF16), moe_w2.astype(BF16)
    router_wp = _pad_lanes(router_w)
    router_bp = _pad_lanes(router_b, NEG)
    dtb = _pad_lanes(dt_bias.reshape(DEPTH, 1, 2 * SSM_HEADS))
    alog = _pad_lanes(a_log.reshape(DEPTH, 1, 2 * SSM_HEADS))
    dsk = jnp.repeat(d_skip, SSM_HEAD_DIM, axis=-1).reshape(DEPTH, 1, D_SSM)
    dr_idx, dc_idx, na_valid = _na_bias_indices()
    na_bias = jnp.where(na_valid, rpb[:, :, dr_idx, dc_idx], NEG)
    cache_k4 = cache_k.reshape(DEC_BATCH, DEPTH, PAST_LEN, D_ATT)
    cache_v4 = cache_v.reshape(DEC_BATCH, DEPTH, PAST_LEN, D_ATT)
    state4 = state_ssm.reshape(DEC_BATCH, DEPTH, 2, D_SSM, SSM_STATE)

    ks, vs, sts = [], [], []
    for layer in range(DEPTH):
        pa, ps, pqkv, pdt = _inproj(x, mod4, norm1_w3, w_in_r, layer)
        params = (conv_a_w[layer], ssm_conv_w[layer], ssm_conv_b[layer].reshape(1, -1), dtb[layer], alog[layer],
                  dsk[layer], ssm_norm_w[layer].reshape(1, D_SSM))
        yab_ctx, st = _mixer_ab(pa, ps, pdt, None, params, SEQ, BATCH, 0)
        yab_lat = _mixer_ab(pa, ps, pdt, state4[:, layer], params, DEC_SEQ, DEC_BATCH, T_CTX // DEC_SEQ)
        yc_ctx = _attn_ctx(pqkv)
        yc_lat = _attn_na(pqkv, cache_k4, cache_v4, na_bias[layer], layer)
        yab = jnp.concatenate([yab_ctx, yab_lat], axis=0)
        yc = jnp.concatenate([yc_ctx, yc_lat], axis=0)
        ks.append(pqkv[:T_CTX, D_ATT:2 * D_ATT].astype(F32).reshape(BATCH, SEQ, ATT_HEADS, ATT_HEAD_DIM))
        vs.append(pqkv[:T_CTX, 2 * D_ATT:].astype(F32).reshape(BATCH, SEQ, ATT_HEADS, ATT_HEAD_DIM))
        sts.append(st.reshape(BATCH, 2, SSM_HEADS, SSM_HEAD_DIM, SSM_STATE))
        i = layer // 2
        if layer % 2 == 0:
            x1, h2 = _outproj(yab, yc, x, mod4, w_out_b, norm2_w3, None, layer)
            x = _ffn(h2, x1, mod4, ffn_w1c[i], ffn_w3c[i], ffn_w2c[i], None, layer)
        else:
            x1, h2, gates = _outproj(yab, yc, x, mod4, w_out_b, norm2_w3, (router_wp[i], router_bp[i][None, :]), layer)
            x = _ffn(h2, x1, mod4, moe_w1b[i], moe_w3b[i], moe_w2b[i], gates, layer)

    y = _final_norm(x, final_norm_w)
    y_prompt = y[:T_CTX].reshape(BATCH, SEQ, D)
    y_sample = y[T_CTX:].reshape(DEC_BATCH, DEC_SEQ, D)
    return (y_prompt, y_sample, jnp.stack(ks, axis=1), jnp.stack(vs, axis=1), jnp.stack(sts, axis=1))
```

```python
import functools

import jax
import jax.numpy as jnp
from jax import lax
from jax.experimental import pallas as pl
from jax.experimental.pallas import tpu as pltpu

F32 = jnp.float32
BF16 = jnp.bfloat16

D = 1024
BATCH = 16
SEQ = 256
DEPTH = 4
DEC_BATCH = 4
DEC_SEQ = 1024
PAST_LEN = 512
GRID_W = 64
D_CONV = 256
D_SSM = 384
SSM_HEADS = 6
SSM_HEAD_DIM = 64
SSM_STATE = 128
CHUNK = 128
N_BC = 512
D_ATT = 384
ATT_HEADS = 6
ATT_HEAD_DIM = 64
NA_ROWS = 8
NA_COLS = 16
D_FF = 2816
N_EXPERTS = 8
D_FF_EXPERT = 1408
N_MOD = 6
EPS = 1e-6

T_CTX = BATCH * SEQ
T_LAT = DEC_BATCH * DEC_SEQ
T_ALL = T_CTX + T_LAT

W_A = 3 * D_CONV
W_S = 2 * D_SSM + N_BC
W_QKV = 3 * D_ATT
W_DT = 128
N_PROJ_PAD = W_A + W_S + W_QKV + W_DT

LANES = 128
NEG = -1e30

NA_TQ = 128
NA_TILES = DEC_SEQ // NA_TQ
NA_WIN_ROWS = 10
NA_WIN = NA_WIN_ROWS * GRID_W

VMEM_LIMIT = 56 * 1024 * 1024


def _cparams(n_axes):
    return pltpu.CompilerParams(dimension_semantics=("arbitrary",) * n_axes, vmem_limit_bytes=VMEM_LIMIT)


def _silu(x):
    return x * jax.nn.sigmoid(x)


def _mod_row(i, tm):
    n_ctx = T_CTX // tm
    per_lat = DEC_SEQ // tm
    return jnp.where(i < n_ctx, 0, (i - n_ctx) // per_lat + 1)


def _mod_spec(layer, k, tm):
    return pl.BlockSpec((None, None, 1, D), lambda i, *_: (layer, _mod_row(i, tm), 0, k))


ADA_TN = 1536


def _adaln_kernel(c_ref, w_ref, b_ref, o_ref):
    s = _silu(c_ref[...]).astype(BF16)
    o_ref[...] = jnp.dot(s, w_ref[...].astype(BF16), preferred_element_type=F32) + b_ref[...]


def _adaln_all(cvec, w_ada, b_ada):
    n = N_MOD * D
    return pl.pallas_call(
        _adaln_kernel,
        grid=(DEPTH, n // ADA_TN),
        in_specs=[
            pl.BlockSpec((8, D), lambda l, j: (0, 0)),
            pl.BlockSpec((None, D, ADA_TN), lambda l, j: (l, 0, j)),
            pl.BlockSpec((None, 1, ADA_TN), lambda l, j: (l, 0, j)),
        ],
        out_specs=pl.BlockSpec((None, 8, ADA_TN), lambda l, j: (l, 0, j)),
        out_shape=jax.ShapeDtypeStruct((DEPTH, 8, n), F32),
        compiler_params=_cparams(2),
        name="adaln",
    )(cvec, w_ada, b_ada.reshape(DEPTH, 1, n))


TM_IN = 512


def _norm_mod(x, nw, sh, sc):
    var = jnp.mean(x * x, axis=-1, keepdims=True)
    h = x * lax.rsqrt(var + EPS) * nw
    return h * (1.0 + sc) + sh


def _inproj_kernel(x_ref, nw_ref, sh_ref, sc_ref, w_ref, pa_ref, ps_ref, pqkv_ref, pdt_ref):
    hb = _norm_mod(x_ref[...], nw_ref[...], sh_ref[...], sc_ref[...]).astype(BF16)
    o = 0
    pa_ref[...] = jnp.dot(hb, w_ref[:, o:o + W_A], preferred_element_type=F32).astype(BF16)
    o += W_A
    ps_ref[...] = jnp.dot(hb, w_ref[:, o:o + W_S], preferred_element_type=F32).astype(BF16)
    o += W_S
    pqkv_ref[...] = jnp.dot(hb, w_ref[:, o:o + W_QKV], preferred_element_type=F32).astype(BF16)
    o += W_QKV
    pdt_ref[...] = jnp.dot(hb, w_ref[:, o:o + W_DT], preferred_element_type=F32)


def _inproj(x, mod4, norm1_w3, w_in_r, layer):
    tm = TM_IN
    row = lambda w: pl.BlockSpec((tm, w), lambda i: (i, 0))
    return pl.pallas_call(
        _inproj_kernel,
        grid=(T_ALL // tm,),
        in_specs=[
            row(D),
            pl.BlockSpec((None, 1, D), lambda i: (layer, 0, 0)),
            _mod_spec(layer, 0, tm),
            _mod_spec(layer, 1, tm),
            pl.BlockSpec((None, D, N_PROJ_PAD), lambda i: (layer, 0, 0)),
        ],
        out_specs=[row(W_A), row(W_S), row(W_QKV), row(W_DT)],
        out_shape=[
            jax.ShapeDtypeStruct((T_ALL, W_A), BF16),
            jax.ShapeDtypeStruct((T_ALL, W_S), BF16),
            jax.ShapeDtypeStruct((T_ALL, W_QKV), BF16),
            jax.ShapeDtypeStruct((T_ALL, W_DT), F32),
        ],
        compiler_params=_cparams(1),
        name="inproj",
    )(x, norm1_w3, mod4, mod4, w_in_r)


HALO = 16


def _conv3(cur, prev_row, next_row, w):
    n = cur.shape[0]
    rid = lax.broadcasted_iota(jnp.int32, cur.shape, 0)
    dn = jnp.where(rid == 0, prev_row, pltpu.roll(cur, 1, 0))
    up = jnp.where(rid == n - 1, next_row, pltpu.roll(cur, n - 1, 0))
    return w[0:1] * dn + w[1:2] * cur + w[2:3] * up


def _cumsum_rows(x):
    rid = lax.broadcasted_iota(jnp.int32, x.shape, 0)
    s = 1
    while s < x.shape[0]:
        x = x + jnp.where(rid >= s, pltpu.roll(x, s, 0), 0.0)
        s *= 2
    return x


def _mixer_ab_kernel(seq_len, has_init, *refs):
    if has_init:
        (pa_ref, ps_ref, pdt_ref, init_ref, wa_ref, cw_ref, cb_ref, dtb_ref, alog_ref, dsk_ref, nw_ref, ctx_ref,
         both_ref, xs_ref, bc_ref, dt_ref, y_ref, rt_ref) = refs
        st_ref = None
        both_ref[0] = ctx_ref[...]
        yab_ref = both_ref.at[1]
    else:
        (pa_ref, ps_ref, pdt_ref, wa_ref, cw_ref, cb_ref, dtb_ref, alog_ref, dsk_ref, nw_ref,
         yab_ref, st_ref, xs_ref, bc_ref, dt_ref, y_ref, rt_ref) = refs
        init_ref = None
    nc = seq_len // CHUNK

    def halo(ref, r0, c):
        lo = jnp.maximum(r0 - HALO, 0)
        hi = jnp.minimum(r0 + CHUNK, seq_len - HALO)
        prev = ref[pl.ds(pl.multiple_of(lo, HALO), HALO), :].astype(F32)[HALO - 1:HALO]
        nxt = ref[pl.ds(pl.multiple_of(hi, HALO), HALO), :].astype(F32)[0:1]
        return jnp.where(c > 0, prev, 0.0), jnp.where(c < nc - 1, nxt, 0.0)

    def prep(c, carry):
        r0 = pl.multiple_of(c * CHUNK, CHUNK)
        rows = pl.ds(r0, CHUNK)
        pa = pa_ref[rows, :].astype(F32)
        pp, pn = halo(pa_ref, r0, c)
        g = pa[:, D_CONV:2 * D_CONV] * pa[:, 2 * D_CONV:]
        gp = pp[:, D_CONV:2 * D_CONV] * pp[:, 2 * D_CONV:]
        gn = pn[:, D_CONV:2 * D_CONV] * pn[:, 2 * D_CONV:]
        ya = pa[:, :D_CONV] * _conv3(g, gp, gn, wa_ref[...])
        yab_ref[rows, 0:D_CONV] = ya.astype(BF16)

        ps = ps_ref[rows, :].astype(F32)
        sp, sn = halo(ps_ref, r0, c)
        cw = cw_ref[...]
        cb = cb_ref[...]
        xs = _conv3(ps[:, :D_SSM], sp[:, :D_SSM], sn[:, :D_SSM], cw[:, :D_SSM]) + cb[:, :D_SSM]
        xs_ref[rows, :] = _silu(xs)
        o = 2 * D_SSM
        bc = _conv3(ps[:, o:], sp[:, o:], sn[:, o:], cw[:, D_SSM:]) + cb[:, D_SSM:]
        bc_ref[rows, :] = _silu(bc)
        t = pdt_ref[rows, :] + dtb_ref[...]
        dt_ref[rows, :] = jnp.maximum(t, 0.0) + jnp.log1p(jnp.exp(-jnp.abs(t)))
        return carry

    lax.fori_loop(0, nc, prep, 0)

    for d in range(2):
        for j in range(D_SSM // LANES):
            cols = slice(j * LANES, (j + 1) * LANES)
            if has_init:
                rt_ref[d, :, cols] = init_ref[d, cols, :].T
            else:
                rt_ref[d, :, cols] = jnp.zeros((SSM_STATE, LANES), F32)

    a_row = -jnp.exp(alog_ref[...])

    def ssd_chunk(c, d):
        rows = pl.ds(pl.multiple_of(c * CHUNK, CHUNK), CHUNK)
        xs = xs_ref[rows, :]
        bc = bc_ref[rows, :]
        dt = dt_ref[rows, :]
        adt = dt * a_row
        cs = _cumsum_rows(adt)
        total = cs[CHUNK - 1:CHUNK, :]
        e = cs if d == 0 else total - cs + adt
        e_t = e.T
        dec = jnp.exp(total - e)
        ee = jnp.exp(e)
        etot = jnp.exp(total)
        ri = lax.broadcasted_iota(jnp.int32, (CHUNK, CHUNK), 0)
        ci = lax.broadcasted_iota(jnp.int32, (CHUNK, CHUNK), 1)
        valid = (ri >= ci) if d == 0 else (ri <= ci)
        for g in range(2):
            bg = bc[:, g * SSM_STATE:(g + 1) * SSM_STATE]
            cg = bc[:, 2 * SSM_STATE + g * SSM_STATE:2 * SSM_STATE + (g + 1) * SSM_STATE].astype(BF16)
            cb = lax.dot_general(cg, bg.astype(BF16), (((1,), (1,)), ((), ())), preferred_element_type=F32)
            bg_t = bg.T.astype(BF16)
            for hh in range(SSM_HEADS // 2):
                h = g * (SSM_HEADS // 2) + hh
                k = d * SSM_HEADS + h
                hc = slice(h * SSM_HEAD_DIM, (h + 1) * SSM_HEAD_DIM)
                xdt = xs[:, hc] * dt[:, k:k + 1]
                seg = e[:, k:k + 1] - e_t[k:k + 1, :]
                lm = jnp.where(valid, jnp.exp(jnp.where(valid, seg, 0.0)), 0.0)
                y_diag = jnp.dot((cb * lm).astype(BF16), xdt.astype(BF16), preferred_element_type=F32)
                rt_h = rt_ref[d, :, hc]
                y_off = jnp.dot(cg, rt_h.astype(BF16), preferred_element_type=F32) * ee[:, k:k + 1]
                if d == 0:
                    y_ref[rows, hc] = y_diag + y_off
                else:
                    y_ref[rows, hc] = y_ref[rows, hc] + y_diag + y_off
                st = jnp.dot(bg_t, (xdt * dec[:, k:k + 1]).astype(BF16), preferred_element_type=F32)
                rt_ref[d, :, hc] = rt_h * etot[:, k:k + 1] + st

    lax.fori_loop(0, nc, lambda i, cr: (ssd_chunk(i, 0), cr)[1], 0)
    lax.fori_loop(0, nc, lambda i, cr: (ssd_chunk(nc - 1 - i, 1), cr)[1], 0)

    if not has_init:
        for d in range(2):
            for j in range(D_SSM // LANES):
                cols = slice(j * LANES, (j + 1) * LANES)
                st_ref[d, cols, :] = rt_ref[d, :, cols].T

    def finish(c, carry):
        rows = pl.ds(pl.multiple_of(c * CHUNK, CHUNK), CHUNK)
        y = y_ref[rows, :] + dsk_ref[...] * xs_ref[rows, :]
        y = y * _silu(ps_ref[rows, D_SSM:2 * D_SSM].astype(F32))
        col = lax.broadcasted_iota(jnp.int32, y.shape, 1)
        first = col < D_SSM // 2
        ysq = y * y
        s0 = jnp.sum(jnp.where(first, ysq, 0.0), axis=-1, keepdims=True)
        s1 = jnp.sum(jnp.where(first, 0.0, ysq), axis=-1, keepdims=True)
        inv = 1.0 / (D_SSM // 2)
        r = jnp.where(first, lax.rsqrt(s0 * inv + EPS), lax.rsqrt(s1 * inv + EPS))
        yab_ref[rows, D_CONV:] = (y * r * nw_ref[...]).astype(BF16)
        return carry

    lax.fori_loop(0, nc, finish, 0)


def _mixer_ab(pa, ps, pdt, init, params, seq_len, n_seq, blk0, layer, yab=None):
    has_init = init is not None
    n_w = D_CONV + D_SSM
    seq = lambda w: pl.BlockSpec((seq_len, w), lambda s: (blk0 + s, 0))
    full = lambda a: pl.BlockSpec(a.shape, lambda s: (0,) * a.ndim)
    in_specs = [seq(W_A), seq(W_S), seq(W_DT)]
    args = [pa, ps, pdt]
    if has_init:
        in_specs.append(pl.BlockSpec((None, None, 2, D_SSM, SSM_STATE), lambda s: (s, layer, 0, 0, 0)))
        args.append(init)
    in_specs += [full(p) for p in params]
    args += list(params)
    if has_init:
        assert T_CTX == T_LAT
        in_specs.append(pl.BlockSpec((seq_len, n_w), lambda s: (s, 0)))
        args.append(yab)
        out_specs = pl.BlockSpec((2, seq_len, n_w), lambda s: (0, s, 0))
        out_shape = jax.ShapeDtypeStruct((2, T_CTX, n_w), BF16)
    else:
        out_specs = [pl.BlockSpec((seq_len, n_w), lambda s: (s, 0)),
                     pl.BlockSpec((None, 2, D_SSM, SSM_STATE), lambda s: (s, 0, 0, 0))]
        out_shape = [jax.ShapeDtypeStruct((T_CTX, n_w), BF16),
                     jax.ShapeDtypeStruct((n_seq, 2, D_SSM, SSM_STATE), F32)]
    return pl.pallas_call(
        functools.partial(_mixer_ab_kernel, seq_len, has_init),
        grid=(n_seq,),
        in_specs=in_specs,
        out_specs=out_specs,
        out_shape=out_shape,
        scratch_shapes=[
            pltpu.VMEM((seq_len, D_SSM), F32),
            pltpu.VMEM((seq_len, N_BC), F32),
            pltpu.VMEM((seq_len, LANES), F32),
            pltpu.VMEM((seq_len, D_SSM), F32),
            pltpu.VMEM((2, SSM_STATE, D_SSM), F32),
        ],
        compiler_params=_cparams(1),
        name="mixer_ab_lat" if has_init else "mixer_ab_ctx",
    )(*args)


ATT_SCALE = ATT_HEAD_DIM ** -0.5


def _dot_nt(a, b):
    return lax.dot_general(a, b, (((1,), (1,)), ((), ())), preferred_element_type=F32)


def _attn_ctx_kernel(qkv_ref, o_ref):
    for h in range(ATT_HEADS):
        hc = slice(h * ATT_HEAD_DIM, (h + 1) * ATT_HEAD_DIM)
        q = qkv_ref[:, h * ATT_HEAD_DIM:(h + 1) * ATT_HEAD_DIM]
        k = qkv_ref[:, D_ATT + h * ATT_HEAD_DIM:D_ATT + (h + 1) * ATT_HEAD_DIM]
        v = qkv_ref[:, 2 * D_ATT + h * ATT_HEAD_DIM:2 * D_ATT + (h + 1) * ATT_HEAD_DIM]
        s = _dot_nt(q, k) * ATT_SCALE
        p = jnp.exp(s - jnp.max(s, axis=-1, keepdims=True))
        l = jnp.sum(p, axis=-1, keepdims=True)
        o = jnp.dot(p.astype(BF16), v, preferred_element_type=F32)
        o_ref[:, hc] = (o / l).astype(BF16)


def _attn_ctx(pqkv):
    return pl.pallas_call(
        _attn_ctx_kernel,
        grid=(BATCH,),
        in_specs=[pl.BlockSpec((SEQ, W_QKV), lambda s: (s, 0))],
        out_specs=pl.BlockSpec((SEQ, D_ATT), lambda s: (s, 0)),
        out_shape=jax.ShapeDtypeStruct((T_CTX, D_ATT), BF16),
        compiler_params=_cparams(1),
        name="attn_ctx",
    )(pqkv)


def _na_win_start(i):
    return jnp.clip(2 * i - NA_ROWS // 2, 0, DEC_SEQ // GRID_W - NA_WIN_ROWS)


NA_BIAS_FLAT = 1024


def _attn_na_kernel(qkv_ref, kc_ref, vc_ref, rpb_ref, ctx_ref, both_ref, bias_ref):
    both_ref[0] = ctx_ref[...]
    o_ref = both_ref.at[1]
    i = pl.program_id(0)
    q0 = pl.multiple_of(i * NA_TQ, NA_TQ)
    w_row = _na_win_start(i)
    w0 = pl.multiple_of(w_row * GRID_W, 2 * GRID_W)

    @pl.when(pl.program_id(1) == 0)
    def _():
        ql = lax.broadcasted_iota(jnp.int32, (NA_TQ, NA_WIN), 0) + q0
        kl = lax.broadcasted_iota(jnp.int32, (NA_TQ, NA_WIN), 1) + w0
        qr, qc = ql // GRID_W, ql % GRID_W
        kr, kc = kl // GRID_W, kl % GRID_W
        rs = jnp.clip(qr - NA_ROWS // 2, 0, DEC_SEQ // GRID_W - NA_ROWS)
        cs = jnp.clip(qc - NA_COLS // 2, 0, GRID_W - NA_COLS)
        valid = (kr >= rs) & (kr < rs + NA_ROWS) & (kc >= cs) & (kc < cs + NA_COLS)
        centre = (NA_ROWS - 1) * GRID_W + NA_COLS - 1
        shift = lax.rem(NA_BIAS_FLAT - (w0 - q0 + centre), NA_BIAS_FLAT)
        for h in range(ATT_HEADS):
            table = jnp.broadcast_to(rpb_ref[h], (NA_TQ, NA_BIAS_FLAT))
            rolled = pltpu.roll(table, shift, 1, stride=1, stride_axis=0)
            bias_ref[h] = jnp.where(valid, rolled[:, :NA_WIN], NEG)

    for h in range(ATT_HEADS):
        hc = slice(h * ATT_HEAD_DIM, (h + 1) * ATT_HEAD_DIM)
        kcs = slice(D_ATT + h * ATT_HEAD_DIM, D_ATT + (h + 1) * ATT_HEAD_DIM)
        vcs = slice(2 * D_ATT + h * ATT_HEAD_DIM, 2 * D_ATT + (h + 1) * ATT_HEAD_DIM)
        q = qkv_ref[pl.ds(q0, NA_TQ), hc]
        k_loc = qkv_ref[pl.ds(w0, NA_WIN), kcs]
        v_loc = qkv_ref[pl.ds(w0, NA_WIN), vcs]
        k_ctx = kc_ref[:, hc].astype(BF16)
        v_ctx = vc_ref[:, hc].astype(BF16)
        s_loc = _dot_nt(q, k_loc) * ATT_SCALE + bias_ref[h]
        s_ctx = _dot_nt(q, k_ctx) * ATT_SCALE
        m = jnp.maximum(jnp.max(s_loc, axis=-1, keepdims=True), jnp.max(s_ctx, axis=-1, keepdims=True))
        p_loc = jnp.exp(s_loc - m)
        p_ctx = jnp.exp(s_ctx - m)
        l = jnp.sum(p_loc, axis=-1, keepdims=True) + jnp.sum(p_ctx, axis=-1, keepdims=True)
        o = (jnp.dot(p_loc.astype(BF16), v_loc, preferred_element_type=F32)
             + jnp.dot(p_ctx.astype(BF16), v_ctx, preferred_element_type=F32))
        o_ref[:, hc] = (o / l).astype(BF16)


def _attn_na(pqkv, cache_k4, cache_v4, rpb_flat, yc_ctx, layer):
    assert T_CTX == T_LAT
    lat0 = T_CTX // DEC_SEQ
    return pl.pallas_call(
        _attn_na_kernel,
        grid=(NA_TILES, DEC_BATCH),
        in_specs=[
            pl.BlockSpec((DEC_SEQ, W_QKV), lambda i, b: (lat0 + b, 0)),
            pl.BlockSpec((None, None, PAST_LEN, D_ATT), lambda i, b: (b, layer, 0, 0)),
            pl.BlockSpec((None, None, PAST_LEN, D_ATT), lambda i, b: (b, layer, 0, 0)),
            pl.BlockSpec((None, ATT_HEADS, 1, NA_BIAS_FLAT), lambda i, b: (layer, 0, 0, 0)),
            pl.BlockSpec((NA_TQ, D_ATT), lambda i, b: (b * NA_TILES + i, 0)),
        ],
        out_specs=pl.BlockSpec((2, NA_TQ, D_ATT), lambda i, b: (0, b * NA_TILES + i, 0)),
        out_shape=jax.ShapeDtypeStruct((2, T_CTX, D_ATT), BF16),
        scratch_shapes=[pltpu.VMEM((ATT_HEADS, NA_TQ, NA_WIN), F32)],
        compiler_params=_cparams(2),
        name="attn_na",
    )(pqkv, cache_k4, cache_v4, rpb_flat, yc_ctx)


TM_OUT = 512


R_E0, R_E1, R_W0, R_W1, R_P0, R_P1 = range(6)


def _outproj_kernel(with_router, *refs):
    if with_router:
        (yab_ref, yc_ref, x_ref, w_ref, g1_ref, nw_ref, sh_ref, sc_ref, rw_ref, rb_ref,
         x1_ref, h2_ref, route_ref, count_ref, base_ref) = refs
    else:
        yab_ref, yc_ref, x_ref, w_ref, g1_ref, nw_ref, sh_ref, sc_ref, x1_ref, h2_ref = refs
    n_ab = D_CONV + D_SSM
    out = (jnp.dot(yab_ref[...], w_ref[0:n_ab, :], preferred_element_type=F32)
           + jnp.dot(yc_ref[...], w_ref[n_ab:, :], preferred_element_type=F32))
    x1 = x_ref[...] + g1_ref[...] * out
    x1_ref[...] = x1
    h2 = _norm_mod(x1, nw_ref[...], sh_ref[...], sc_ref[...])
    h2_ref[...] = h2.astype(h2_ref.dtype)
    if with_router:
        @pl.when(pl.program_id(0) == 0)
        def _():
            base_ref[...] = jnp.zeros_like(base_ref)

        logits = jnp.dot(h2, rw_ref[...], preferred_element_type=F32, precision=lax.Precision.HIGHEST)
        logits = logits + rb_ref[...]
        lane = lax.broadcasted_iota(jnp.int32, logits.shape, 1)
        m1 = jnp.max(logits, axis=-1, keepdims=True)
        i1 = jnp.min(jnp.where(logits == m1, lane, LANES), axis=-1, keepdims=True)
        rest = jnp.where(lane == i1, -jnp.inf, logits)
        m2 = jnp.max(rest, axis=-1, keepdims=True)
        i2 = jnp.min(jnp.where(rest == m2, lane, LANES), axis=-1, keepdims=True)
        e2 = jnp.exp(m2 - m1)
        den = 1.0 + e2
        tm = logits.shape[0]
        sel0, sel1 = lane == i1, lane == i2
        picked = jnp.where(sel0 | sel1, 1.0, 0.0)
        earlier = (lax.broadcasted_iota(jnp.int32, (tm, tm), 1)
                   < lax.broadcasted_iota(jnp.int32, (tm, tm), 0)).astype(BF16)
        rank = base_ref[...] + jnp.dot(earlier, picked.astype(BF16), preferred_element_type=F32)
        p0 = jnp.sum(jnp.where(sel0, rank, 0.0), axis=-1, keepdims=True)
        p1 = jnp.sum(jnp.where(sel1, rank, 0.0), axis=-1, keepdims=True)
        base = base_ref[...] + jnp.sum(picked, axis=0, keepdims=True)
        base_ref[...] = base
        count_ref[...] = base
        rec = jnp.zeros(logits.shape, F32)
        for k, val in ((R_E0, i1.astype(F32)), (R_E1, i2.astype(F32)), (R_W0, 1.0 / den), (R_W1, e2 / den),
                       (R_P0, p0), (R_P1, p1)):
            rec = jnp.where(lane == k, val, rec)
        route_ref[...] = rec


def _outproj(yab, yc, x, mod4, w_out_b, norm2_w3, router, layer):
    tm = TM_OUT
    with_router = router is not None
    row = lambda w: pl.BlockSpec((tm, w), lambda i: (i, 0))
    in_specs = [
        row(D_CONV + D_SSM), row(D_ATT), row(D),
        pl.BlockSpec((None, D, D), lambda i: (layer, 0, 0)),
        _mod_spec(layer, 2, tm),
        pl.BlockSpec((None, 1, D), lambda i: (layer, 0, 0)),
        _mod_spec(layer, 3, tm),
        _mod_spec(layer, 4, tm),
    ]
    args = [yab, yc, x, w_out_b, mod4, norm2_w3, mod4, mod4]
    out_specs = [row(D), row(D)]
    out_shape = [jax.ShapeDtypeStruct((T_ALL, D), F32), jax.ShapeDtypeStruct((T_ALL, D), F32 if with_router else BF16)]
    scratch = []
    if with_router:
        rw, rb = router
        in_specs += [pl.BlockSpec(rw.shape, lambda i: (0, 0)), pl.BlockSpec(rb.shape, lambda i: (0, 0))]
        args += [rw, rb]
        out_specs += [row(LANES), pl.BlockSpec((1, LANES), lambda i: (0, 0))]
        out_shape += [jax.ShapeDtypeStruct((T_ALL, LANES), F32), jax.ShapeDtypeStruct((1, LANES), F32)]
        scratch = [pltpu.VMEM((1, LANES), F32)]
    return pl.pallas_call(
        functools.partial(_outproj_kernel, with_router),
        grid=(T_ALL // tm,),
        in_specs=in_specs,
        out_specs=out_specs,
        out_shape=out_shape,
        scratch_shapes=scratch,
        compiler_params=_cparams(1),
        name="outproj_router" if with_router else "outproj",
    )(*args)


TM_FF = 512


def _swiglu(h, w1, w3, w2):
    u = jnp.dot(h, w1, preferred_element_type=F32)
    v = jnp.dot(h, w3, preferred_element_type=F32)
    return jnp.dot((_silu(u) * v).astype(BF16), w2, preferred_element_type=F32)


def _ffn_kernel(h_ref, w1_ref, w3_ref, w2_ref, x_ref, g2_ref, o_ref, acc_ref):
    e = pl.program_id(1)

    @pl.when(e == 0)
    def _():
        acc_ref[...] = jnp.zeros_like(acc_ref)

    acc_ref[...] += _swiglu(h_ref[...], w1_ref[...], w3_ref[...], w2_ref[...])

    @pl.when(e == pl.num_programs(1) - 1)
    def _():
        o_ref[...] = x_ref[...] + g2_ref[...] * acc_ref[...]


def _ffn(h2, x1, mod4, w1, w3, w2, layer):
    tm = TM_FF
    n_chunks = w1.shape[0]
    row = lambda w: pl.BlockSpec((tm, w), lambda i, e: (i, 0))
    return pl.pallas_call(
        _ffn_kernel,
        grid=(T_ALL // tm, n_chunks),
        in_specs=[
            row(D),
            pl.BlockSpec((None, D, D_FF_EXPERT), lambda i, e: (e, 0, 0)),
            pl.BlockSpec((None, D, D_FF_EXPERT), lambda i, e: (e, 0, 0)),
            pl.BlockSpec((None, D_FF_EXPERT, D), lambda i, e: (e, 0, 0)),
            row(D),
            _mod_spec(layer, 5, tm),
        ],
        out_specs=row(D),
        out_shape=jax.ShapeDtypeStruct((T_ALL, D), F32),
        scratch_shapes=[pltpu.VMEM((tm, D), F32)],
        compiler_params=_cparams(2),
        name="ffn",
    )(h2, w1, w3, w2, x1, mod4)


TOP_K = 2
TM_X = 512
TM_R = 512
N_SLOTS = TOP_K * T_ALL + N_EXPERTS * TM_X
N_XTILES = N_SLOTS // TM_X


def _route_plan(route, counts):
    cnt = counts[0, :N_EXPERTS].astype(jnp.int32)
    gsz = (cnt + TM_X - 1) // TM_X * TM_X
    ends = jnp.cumsum(gsz)
    offs = ends - gsz
    e = route[:, R_E0:R_E1 + 1].astype(jnp.int32)
    pos = route[:, R_P0:R_P1 + 1].astype(jnp.int32)
    sel = e[..., None] == jnp.arange(N_EXPERTS, dtype=jnp.int32)
    dest = jnp.sum(jnp.where(sel, offs, 0), axis=-1) + pos
    n_tiles = ends[-1] // TM_X
    tile = jnp.minimum(jnp.arange(N_XTILES, dtype=jnp.int32), n_tiles - 1)
    texp = jnp.sum((ends[None, :] <= (tile * TM_X)[:, None]).astype(jnp.int32), axis=-1)
    return (dest.reshape(T_ALL // TM_R, 1, TOP_K * TM_R).astype(jnp.int32), tile, texp.astype(jnp.int32),
            n_tiles.reshape(1).astype(jnp.int32))


def _row_copy(src, s, dst, d, sem):
    return pltpu.make_async_copy(src.at[pl.ds(s, 1)], dst.at[pl.ds(d, 1)], sem)


def _dispatch_kernel(dest_ref, h_ref, xs_in_ref, xs_ref, sem):
    del xs_in_ref

    def issue(t, carry):
        for k in range(TOP_K):
            _row_copy(h_ref, t, xs_ref, dest_ref[0, TOP_K * t + k], sem).start()
        return carry

    lax.fori_loop(0, TM_R, issue, 0)

    def drain(t, carry):
        for k in range(TOP_K):
            _row_copy(h_ref, 0, xs_ref, 0, sem).wait()
        return carry

    lax.fori_loop(0, TM_R, drain, 0)


def _dispatch(dest, h2):
    xs0 = jnp.zeros((N_SLOTS, D), F32)
    return pl.pallas_call(
        _dispatch_kernel,
        grid=(T_ALL // TM_R,),
        in_specs=[
            pl.BlockSpec((None, 1, TOP_K * TM_R), lambda i: (i, 0, 0), memory_space=pltpu.SMEM),
            pl.BlockSpec((TM_R, D), lambda i: (i, 0)),
            pl.BlockSpec(memory_space=pl.ANY),
        ],
        out_specs=pl.BlockSpec(memory_space=pl.ANY),
        out_shape=jax.ShapeDtypeStruct((N_SLOTS, D), F32),
        scratch_shapes=[pltpu.SemaphoreType.DMA],
        input_output_aliases={2: 0},
        compiler_params=_cparams(1),
        name="moe_dispatch",
    )(dest, h2, xs0)


def _experts_kernel(tile_ref, texp_ref, nt_ref, xs_ref, w1_ref, w3_ref, w2_ref, ys_ref):
    del tile_ref, texp_ref
    live = pl.program_id(0) < nt_ref[0]

    @pl.when(live)
    def _():
        ys_ref[...] = _swiglu(xs_ref[...].astype(BF16), w1_ref[...], w3_ref[...], w2_ref[...])

    @pl.when(jnp.logical_not(live))
    def _():
        ys_ref[...] = jnp.zeros_like(ys_ref)


def _experts(tile, texp, n_tiles, xs, w1, w3, w2):
    wspec = lambda shape: pl.BlockSpec((None,) + shape, lambda j, tile, texp, nt: (texp[j], 0, 0))
    rows_in = pl.BlockSpec((TM_X, D), lambda j, tile, texp, nt: (tile[j], 0))
    rows_out = pl.BlockSpec((TM_X, D), lambda j, tile, texp, nt: (j, 0))
    return pl.pallas_call(
        _experts_kernel,
        grid_spec=pltpu.PrefetchScalarGridSpec(
            num_scalar_prefetch=3,
            grid=(N_XTILES,),
            in_specs=[rows_in, wspec((D, D_FF_EXPERT)), wspec((D, D_FF_EXPERT)), wspec((D_FF_EXPERT, D))],
            out_specs=rows_out,
        ),
        out_shape=jax.ShapeDtypeStruct((N_SLOTS, D), F32),
        compiler_params=_cparams(1),
        name="moe_experts",
    )(tile, texp, n_tiles, xs, w1, w3, w2)


def _combine_kernel(dest_ref, ys_ref, route_ref, x_ref, g2_ref, o_ref, y_ref, sem):
    def issue(t, carry):
        for k in range(TOP_K):
            _row_copy(ys_ref, dest_ref[0, TOP_K * t + k], y_ref.at[k], t, sem).start()
        return carry

    lax.fori_loop(0, TM_R, issue, 0)

    def drain(t, carry):
        for k in range(TOP_K):
            _row_copy(ys_ref, 0, y_ref.at[k], 0, sem).wait()
        return carry

    lax.fori_loop(0, TM_R, drain, 0)
    route = route_ref[...]
    moe = route[:, R_W0:R_W0 + 1] * y_ref[0] + route[:, R_W1:R_W1 + 1] * y_ref[1]
    o_ref[...] = x_ref[...] + g2_ref[...] * moe


def _combine(dest, ys, route, x1, mod4, layer):
    row = lambda w: pl.BlockSpec((TM_R, w), lambda i: (i, 0))
    return pl.pallas_call(
        _combine_kernel,
        grid=(T_ALL // TM_R,),
        in_specs=[
            pl.BlockSpec((None, 1, TOP_K * TM_R), lambda i: (i, 0, 0), memory_space=pltpu.SMEM),
            pl.BlockSpec(memory_space=pl.ANY),
            row(LANES),
            row(D),
            _mod_spec(layer, 5, TM_R),
        ],
        out_specs=row(D),
        out_shape=jax.ShapeDtypeStruct((T_ALL, D), F32),
        scratch_shapes=[pltpu.VMEM((TOP_K, TM_R, D), F32), pltpu.SemaphoreType.DMA],
        compiler_params=_cparams(1),
        name="moe_combine",
    )(dest, ys, route, x1, mod4)


def _moe(h2, x1, route, counts, mod4, w1, w3, w2, layer):
    dest, tile, texp, n_tiles = _route_plan(route, counts)
    xs = _dispatch(dest, h2)
    ys = _experts(tile, texp, n_tiles, xs, w1, w3, w2)
    return _combine(dest, ys, route, x1, mod4, layer)


TM_NORM = 1024


def _final_norm_kernel(x_ref, w_ref, o_ref):
    x = x_ref[...]
    var = jnp.mean(x * x, axis=-1, keepdims=True)
    o_ref[...] = x * lax.rsqrt(var + EPS) * w_ref[...]


def _final_norm(x, w):
    return pl.pallas_call(
        _final_norm_kernel,
        grid=(T_ALL // TM_NORM,),
        in_specs=[pl.BlockSpec((TM_NORM, D), lambda i: (i, 0)), pl.BlockSpec((1, D), lambda i: (0, 0))],
        out_specs=pl.BlockSpec((TM_NORM, D), lambda i: (i, 0)),
        out_shape=jax.ShapeDtypeStruct((T_ALL, D), F32),
        compiler_params=_cparams(1),
        name="final_norm",
    )(x, w.reshape(1, D))


def _pad_lanes(a, fill=0.0):
    pad = [(0, 0)] * (a.ndim - 1) + [(0, LANES - a.shape[-1])]
    return jnp.pad(a, pad, constant_values=fill)


def _reorder_w_in(w_in):
    a_end = W_A
    s_end = a_end + W_S
    dt_end = s_end + 2 * SSM_HEADS
    w_dt = jnp.pad(w_in[:, :, s_end:dt_end], ((0, 0), (0, 0), (0, W_DT - 2 * SSM_HEADS)))
    return jnp.concatenate([w_in[:, :, :s_end], w_in[:, :, dt_end:], w_dt], axis=-1).astype(BF16)


def kernel(x_prompt, x_sample, cache_k, cache_v, state_ssm, c, c_ctx, norm1_w, norm2_w, w_ada, b_ada, w_in, w_out, conv_a_w, ssm_conv_w, ssm_conv_b, dt_bias, a_log, d_skip, ssm_norm_w, rpb, ffn_w1, ffn_w3, ffn_w2, router_w, router_b, moe_w1, moe_w3, moe_w2, final_norm_w):
    x = jnp.concatenate([x_prompt.reshape(T_CTX, D), x_sample.reshape(T_LAT, D)], axis=0)

    cvec = jnp.concatenate([c_ctx[None, :], c, jnp.zeros((8 - 1 - DEC_BATCH, D), F32)], axis=0)
    mod4 = _adaln_all(cvec, w_ada, b_ada).reshape(DEPTH, 8, 1, N_MOD * D)

    w_in_r = _reorder_w_in(w_in)
    w_out_b = w_out.astype(BF16)
    norm1_w3 = norm1_w.reshape(DEPTH, 1, D)
    norm2_w3 = norm2_w.reshape(DEPTH, 1, D)
    n_dense = ffn_w1.shape[0]
    split_cols = lambda w: jnp.transpose(w.reshape(n_dense, D, 2, D_FF_EXPERT), (0, 2, 1, 3)).astype(BF16)
    ffn_w1c, ffn_w3c = split_cols(ffn_w1), split_cols(ffn_w3)
    ffn_w2c = ffn_w2.reshape(n_dense, 2, D_FF_EXPERT, D).astype(BF16)
    moe_w1b, moe_w3b, moe_w2b = moe_w1.astype(BF16), moe_w3.astype(BF16), moe_w2.astype(BF16)
    router_wp = _pad_lanes(router_w)
    router_bp = _pad_lanes(router_b, NEG)
    dtb = _pad_lanes(dt_bias.reshape(DEPTH, 1, 2 * SSM_HEADS))
    alog = _pad_lanes(a_log.reshape(DEPTH, 1, 2 * SSM_HEADS))
    dsk = jnp.repeat(d_skip, SSM_HEAD_DIM, axis=-1).reshape(DEPTH, 1, D_SSM)
    rpb_flat = jnp.pad(rpb, ((0, 0), (0, 0), (0, 1), (0, GRID_W - rpb.shape[-1])))
    rpb_flat = rpb_flat.reshape(DEPTH, ATT_HEADS, 1, NA_BIAS_FLAT)
    cache_k4 = cache_k.reshape(DEC_BATCH, DEPTH, PAST_LEN, D_ATT)
    cache_v4 = cache_v.reshape(DEC_BATCH, DEPTH, PAST_LEN, D_ATT)
    state5 = state_ssm.reshape(DEC_BATCH, DEPTH, 2, D_SSM, SSM_STATE)

    ks, vs, sts = [], [], []
    for layer in range(DEPTH):
        pa, ps, pqkv, pdt = _inproj(x, mod4, norm1_w3, w_in_r, layer)
        params = (conv_a_w[layer], ssm_conv_w[layer], ssm_conv_b[layer].reshape(1, -1), dtb[layer], alog[layer],
                  dsk[layer], ssm_norm_w[layer].reshape(1, D_SSM))
        yab, st = _mixer_ab(pa, ps, pdt, None, params, SEQ, BATCH, 0, layer)
        yab = _mixer_ab(pa, ps, pdt, state5, params, DEC_SEQ, DEC_BATCH, T_CTX // DEC_SEQ, layer, yab)
        yab = yab.reshape(T_ALL, D_CONV + D_SSM)
        yc = _attn_na(pqkv, cache_k4, cache_v4, rpb_flat, _attn_ctx(pqkv), layer).reshape(T_ALL, D_ATT)
        ks.append(pqkv[:T_CTX, D_ATT:2 * D_ATT].astype(F32).reshape(BATCH, SEQ, ATT_HEADS, ATT_HEAD_DIM))
        vs.append(pqkv[:T_CTX, 2 * D_ATT:].astype(F32).reshape(BATCH, SEQ, ATT_HEADS, ATT_HEAD_DIM))
        sts.append(st.reshape(BATCH, 2, SSM_HEADS, SSM_HEAD_DIM, SSM_STATE))
        i = layer // 2
        if layer % 2 == 0:
            x1, h2 = _outproj(yab, yc, x, mod4, w_out_b, norm2_w3, None, layer)
            x = _ffn(h2, x1, mod4, ffn_w1c[i], ffn_w3c[i], ffn_w2c[i], layer)
        else:
            x1, h2, route, counts = _outproj(yab, yc, x, mod4, w_out_b, norm2_w3,
                                             (router_wp[i], router_bp[i][None, :]), layer)
            x = _moe(h2, x1, route, counts, mod4, moe_w1b[i], moe_w3b[i], moe_w2b[i], layer)

    y = _final_norm(x, final_norm_w)
    y_prompt = y[:T_CTX].reshape(BATCH, SEQ, D)
    y_sample = y[T_CTX:].reshape(DEC_BATCH, DEC_SEQ, D)
    return (y_prompt, y_sample, jnp.stack(ks, axis=1), jnp.stack(vs, axis=1), jnp.stack(sts, axis=1))
```

```python
import functools

import jax
import jax.numpy as jnp
from jax import lax
from jax.experimental import pallas as pl
from jax.experimental.pallas import tpu as pltpu

F32 = jnp.float32
BF16 = jnp.bfloat16

D = 1024
BATCH = 16
SEQ = 256
DEPTH = 4
DEC_BATCH = 4
DEC_SEQ = 1024
PAST_LEN = 512
GRID_W = 64
D_CONV = 256
D_SSM = 384
SSM_HEADS = 6
SSM_HEAD_DIM = 64
SSM_STATE = 128
CHUNK = 128
N_BC = 512
D_ATT = 384
ATT_HEADS = 6
ATT_HEAD_DIM = 64
NA_ROWS = 8
NA_COLS = 16
D_FF = 2816
N_EXPERTS = 8
D_FF_EXPERT = 1408
N_MOD = 6
EPS = 1e-6

T_CTX = BATCH * SEQ
T_LAT = DEC_BATCH * DEC_SEQ
T_ALL = T_CTX + T_LAT

W_A = 3 * D_CONV
W_S = 2 * D_SSM + N_BC
W_QKV = 3 * D_ATT
W_DT = 128
N_PROJ_PAD = W_A + W_S + W_QKV + W_DT

LANES = 128
NEG = -1e30

NA_TQ = 128
NA_TILES = DEC_SEQ // NA_TQ
NA_WIN_ROWS = 10
NA_WIN = NA_WIN_ROWS * GRID_W

VMEM_LIMIT = 56 * 1024 * 1024


def _cparams(n_axes):
    return pltpu.CompilerParams(dimension_semantics=("arbitrary",) * n_axes, vmem_limit_bytes=VMEM_LIMIT)


def _silu(x):
    return x * jax.nn.sigmoid(x)


def _mod_row(i, tm):
    n_ctx = T_CTX // tm
    per_lat = DEC_SEQ // tm
    return jnp.where(i < n_ctx, 0, (i - n_ctx) // per_lat + 1)


def _mod_spec(layer, k, tm):
    return pl.BlockSpec((None, None, 1, D), lambda i, *_: (layer, _mod_row(i, tm), 0, k))


ADA_TN = 1536


def _adaln_kernel(c_ref, w_ref, b_ref, o_ref):
    s = _silu(c_ref[...]).astype(BF16)
    o_ref[...] = jnp.dot(s, w_ref[...].astype(BF16), preferred_element_type=F32) + b_ref[...]


def _adaln_all(cvec, w_ada, b_ada):
    n = N_MOD * D
    return pl.pallas_call(
        _adaln_kernel,
        grid=(DEPTH, n // ADA_TN),
        in_specs=[
            pl.BlockSpec((8, D), lambda l, j: (0, 0)),
            pl.BlockSpec((None, D, ADA_TN), lambda l, j: (l, 0, j)),
            pl.BlockSpec((None, 1, ADA_TN), lambda l, j: (l, 0, j)),
        ],
        out_specs=pl.BlockSpec((None, 8, ADA_TN), lambda l, j: (l, 0, j)),
        out_shape=jax.ShapeDtypeStruct((DEPTH, 8, n), F32),
        compiler_params=_cparams(2),
        name="adaln",
    )(cvec, w_ada, b_ada.reshape(DEPTH, 1, n))


TM_IN = 512


def _norm_mod(x, nw, sh, sc):
    var = jnp.mean(x * x, axis=-1, keepdims=True)
    h = x * lax.rsqrt(var + EPS) * nw
    return h * (1.0 + sc) + sh


def _inproj_kernel(x_ref, nw_ref, sh_ref, sc_ref, w_ref, pa_ref, ps_ref, pqkv_ref, pdt_ref):
    hb = _norm_mod(x_ref[...], nw_ref[...], sh_ref[...], sc_ref[...]).astype(BF16)
    o = 0
    pa_ref[...] = jnp.dot(hb, w_ref[:, o:o + W_A], preferred_element_type=F32).astype(BF16)
    o += W_A
    ps_ref[...] = jnp.dot(hb, w_ref[:, o:o + W_S], preferred_element_type=F32).astype(BF16)
    o += W_S
    pqkv_ref[...] = jnp.dot(hb, w_ref[:, o:o + W_QKV], preferred_element_type=F32).astype(BF16)
    o += W_QKV
    pdt_ref[...] = jnp.dot(hb, w_ref[:, o:o + W_DT], preferred_element_type=F32)


def _inproj(x, mod4, norm1_w3, w_in_r, layer):
    tm = TM_IN
    row = lambda w: pl.BlockSpec((tm, w), lambda i: (i, 0))
    return pl.pallas_call(
        _inproj_kernel,
        grid=(T_ALL // tm,),
        in_specs=[
            row(D),
            pl.BlockSpec((None, 1, D), lambda i: (layer, 0, 0)),
            _mod_spec(layer, 0, tm),
            _mod_spec(layer, 1, tm),
            pl.BlockSpec((None, D, N_PROJ_PAD), lambda i: (layer, 0, 0)),
        ],
        out_specs=[row(W_A), row(W_S), row(W_QKV), row(W_DT)],
        out_shape=[
            jax.ShapeDtypeStruct((T_ALL, W_A), BF16),
            jax.ShapeDtypeStruct((T_ALL, W_S), BF16),
            jax.ShapeDtypeStruct((T_ALL, W_QKV), BF16),
            jax.ShapeDtypeStruct((T_ALL, W_DT), F32),
        ],
        compiler_params=_cparams(1),
        name="inproj",
    )(x, norm1_w3, mod4, mod4, w_in_r)


HALO = 16


def _conv3(cur, prev_row, next_row, w):
    n = cur.shape[0]
    rid = lax.broadcasted_iota(jnp.int32, cur.shape, 0)
    dn = jnp.where(rid == 0, prev_row, pltpu.roll(cur, 1, 0))
    up = jnp.where(rid == n - 1, next_row, pltpu.roll(cur, n - 1, 0))
    return w[0:1] * dn + w[1:2] * cur + w[2:3] * up


def _cumsum_rows(x):
    rid = lax.broadcasted_iota(jnp.int32, x.shape, 0)
    s = 1
    while s < x.shape[0]:
        x = x + jnp.where(rid >= s, pltpu.roll(x, s, 0), 0.0)
        s *= 2
    return x


def _spread_exact(x, onehot):
    hi = x.astype(BF16)
    rest = x - hi.astype(F32)
    mid = rest.astype(BF16)
    lo = (rest - mid.astype(F32)).astype(BF16)
    return (jnp.dot(hi, onehot, preferred_element_type=F32) + jnp.dot(mid, onehot, preferred_element_type=F32)
            + jnp.dot(lo, onehot, preferred_element_type=F32))


def _mixer_ab_kernel(seq_len, has_init, *refs):
    if has_init:
        (pa_ref, ps_ref, pdt_ref, init_ref, wa_ref, cw_ref, cb_ref, dtb_ref, alog_ref, dsk_ref, nw_ref, ctx_ref,
         both_ref, xs_ref, bc_ref, dt_ref, y_ref, rt_ref) = refs
        st_ref = None
        both_ref[0] = ctx_ref[...]
        yab_ref = both_ref.at[1]
    else:
        (pa_ref, ps_ref, pdt_ref, wa_ref, cw_ref, cb_ref, dtb_ref, alog_ref, dsk_ref, nw_ref,
         yab_ref, st_ref, xs_ref, bc_ref, dt_ref, y_ref, rt_ref) = refs
        init_ref = None
    nc = seq_len // CHUNK

    def halo(ref, r0, c):
        lo = jnp.maximum(r0 - HALO, 0)
        hi = jnp.minimum(r0 + CHUNK, seq_len - HALO)
        prev = ref[pl.ds(pl.multiple_of(lo, HALO), HALO), :].astype(F32)[HALO - 1:HALO]
        nxt = ref[pl.ds(pl.multiple_of(hi, HALO), HALO), :].astype(F32)[0:1]
        return jnp.where(c > 0, prev, 0.0), jnp.where(c < nc - 1, nxt, 0.0)

    def prep(c, carry):
        r0 = pl.multiple_of(c * CHUNK, CHUNK)
        rows = pl.ds(r0, CHUNK)
        pa = pa_ref[rows, :].astype(F32)
        pp, pn = halo(pa_ref, r0, c)
        g = pa[:, D_CONV:2 * D_CONV] * pa[:, 2 * D_CONV:]
        gp = pp[:, D_CONV:2 * D_CONV] * pp[:, 2 * D_CONV:]
        gn = pn[:, D_CONV:2 * D_CONV] * pn[:, 2 * D_CONV:]
        ya = pa[:, :D_CONV] * _conv3(g, gp, gn, wa_ref[...])
        yab_ref[rows, 0:D_CONV] = ya.astype(BF16)

        ps = ps_ref[rows, :].astype(F32)
        sp, sn = halo(ps_ref, r0, c)
        cw = cw_ref[...]
        cb = cb_ref[...]
        xs = _conv3(ps[:, :D_SSM], sp[:, :D_SSM], sn[:, :D_SSM], cw[:, :D_SSM]) + cb[:, :D_SSM]
        xs_ref[rows, :] = _silu(xs)
        o = 2 * D_SSM
        bc = _conv3(ps[:, o:], sp[:, o:], sn[:, o:], cw[:, D_SSM:]) + cb[:, D_SSM:]
        bc_ref[rows, :] = _silu(bc)
        t = pdt_ref[rows, :] + dtb_ref[...]
        dt_ref[rows, :] = jnp.maximum(t, 0.0) + jnp.log1p(jnp.exp(-jnp.abs(t)))
        return carry

    lax.fori_loop(0, nc, prep, 0)

    for d in range(2):
        for j in range(D_SSM // LANES):
            cols = slice(j * LANES, (j + 1) * LANES)
            if has_init:
                rt_ref[d, :, cols] = init_ref[d, cols, :].T
            else:
                rt_ref[d, :, cols] = jnp.zeros((SSM_STATE, LANES), F32)

    a_row = -jnp.exp(alog_ref[...])

    def ssd_chunk(c, d):
        rows = pl.ds(pl.multiple_of(c * CHUNK, CHUNK), CHUNK)
        xs = xs_ref[rows, :]
        bc = bc_ref[rows, :]
        dt = dt_ref[rows, :]
        adt = dt * a_row
        cs = _cumsum_rows(adt)
        total = cs[CHUNK - 1:CHUNK, :]
        e = cs if d == 0 else total - cs + adt
        e_t = e.T
        dec = jnp.exp(total - e)
        ee = jnp.exp(e)
        lane_k = lax.broadcasted_iota(jnp.int32, (LANES, D_SSM), 0)
        to_cols = jnp.where(lane_k == d * SSM_HEADS + lax.broadcasted_iota(jnp.int32, (LANES, D_SSM), 1)
                            // SSM_HEAD_DIM, 1.0, 0.0).astype(BF16)
        lane_k2 = lax.broadcasted_iota(jnp.int32, (LANES, SSM_HEADS * CHUNK), 0)
        to_blocks = jnp.where(lane_k2 == d * SSM_HEADS
                              + lax.broadcasted_iota(jnp.int32, (LANES, SSM_HEADS * CHUNK), 1) // CHUNK,
                              1.0, 0.0).astype(BF16)
        spread = _spread_exact(jnp.concatenate([dt, ee, dec], axis=0), to_cols)
        dt_x, ee_x, dec_x = spread[:CHUNK], spread[CHUNK:2 * CHUNK], spread[2 * CHUNK:]
        etot_x = ee_x[CHUNK - 1:CHUNK] if d == 0 else ee_x[0:1]
        e_blk = _spread_exact(e, to_blocks)
        xdt = xs * dt_x
        xdt_b = xdt.astype(BF16)
        xdd_b = (xdt * dec_x).astype(BF16)
        rt = rt_ref[d]
        rt_b = rt.astype(BF16)
        ri = lax.broadcasted_iota(jnp.int32, (CHUNK, CHUNK), 0)
        ci = lax.broadcasted_iota(jnp.int32, (CHUNK, CHUNK), 1)
        valid = (ri >= ci) if d == 0 else (ri <= ci)
        col_head = lax.broadcasted_iota(jnp.int32, (CHUNK, D_SSM), 1) // SSM_HEAD_DIM
        heads_per_group = SSM_HEADS // 2
        y_diag = jnp.zeros((CHUNK, D_SSM), F32)
        y_off = jnp.zeros((CHUNK, D_SSM), F32)
        st = jnp.zeros((SSM_STATE, D_SSM), F32)
        for g in range(2):
            in_group = col_head // heads_per_group == g
            bg = bc[:, g * SSM_STATE:(g + 1) * SSM_STATE]
            cg = bc[:, 2 * SSM_STATE + g * SSM_STATE:2 * SSM_STATE + (g + 1) * SSM_STATE].astype(BF16)
            cb = lax.dot_general(cg, bg.astype(BF16), (((1,), (1,)), ((), ())), preferred_element_type=F32)
            bg_t = bg.T.astype(BF16)
            y_off = y_off + jnp.dot(cg, jnp.where(in_group, rt_b, jnp.zeros_like(rt_b)),
                                    preferred_element_type=F32)
            st = jnp.where(in_group, jnp.dot(bg_t, xdd_b, preferred_element_type=F32), st)
            for hh in range(heads_per_group):
                h = g * heads_per_group + hh
                k = d * SSM_HEADS + h
                seg = e_blk[:, h * CHUNK:(h + 1) * CHUNK] - e_t[k:k + 1, :]
                lm = jnp.where(valid, jnp.exp(jnp.where(valid, seg, 0.0)), 0.0)
                y_diag = y_diag + jnp.dot((cb * lm).astype(BF16),
                                          jnp.where(col_head == h, xdt_b, jnp.zeros_like(xdt_b)),
                                          preferred_element_type=F32)
        y = y_diag + y_off * ee_x
        if d == 0:
            y_ref[rows, :] = y
        else:
            y_ref[rows, :] = y_ref[rows, :] + y
        rt_ref[d] = rt * etot_x + st

    lax.fori_loop(0, nc, lambda i, cr: (ssd_chunk(i, 0), cr)[1], 0)
    lax.fori_loop(0, nc, lambda i, cr: (ssd_chunk(nc - 1 - i, 1), cr)[1], 0)

    if not has_init:
        for d in range(2):
            for j in range(D_SSM // LANES):
                cols = slice(j * LANES, (j + 1) * LANES)
                st_ref[d, cols, :] = rt_ref[d, :, cols].T

    def finish(c, carry):
        rows = pl.ds(pl.multiple_of(c * CHUNK, CHUNK), CHUNK)
        y = y_ref[rows, :] + dsk_ref[...] * xs_ref[rows, :]
        y = y * _silu(ps_ref[rows, D_SSM:2 * D_SSM].astype(F32))
        col = lax.broadcasted_iota(jnp.int32, y.shape, 1)
        first = col < D_SSM // 2
        ysq = y * y
        s0 = jnp.sum(jnp.where(first, ysq, 0.0), axis=-1, keepdims=True)
        s1 = jnp.sum(jnp.where(first, 0.0, ysq), axis=-1, keepdims=True)
        inv = 1.0 / (D_SSM // 2)
        r = jnp.where(first, lax.rsqrt(s0 * inv + EPS), lax.rsqrt(s1 * inv + EPS))
        yab_ref[rows, D_CONV:] = (y * r * nw_ref[...]).astype(BF16)
        return carry

    lax.fori_loop(0, nc, finish, 0)


def _mixer_ab(pa, ps, pdt, init, params, seq_len, n_seq, blk0, layer, yab=None):
    has_init = init is not None
    n_w = D_CONV + D_SSM
    seq = lambda w: pl.BlockSpec((seq_len, w), lambda s: (blk0 + s, 0))
    full = lambda a: pl.BlockSpec(a.shape, lambda s: (0,) * a.ndim)
    in_specs = [seq(W_A), seq(W_S), seq(W_DT)]
    args = [pa, ps, pdt]
    if has_init:
        in_specs.append(pl.BlockSpec((None, None, 2, D_SSM, SSM_STATE), lambda s: (s, layer, 0, 0, 0)))
        args.append(init)
    in_specs += [full(p) for p in params]
    args += list(params)
    if has_init:
        assert T_CTX == T_LAT
        in_specs.append(pl.BlockSpec((seq_len, n_w), lambda s: (s, 0)))
        args.append(yab)
        out_specs = pl.BlockSpec((2, seq_len, n_w), lambda s: (0, s, 0))
        out_shape = jax.ShapeDtypeStruct((2, T_CTX, n_w), BF16)
    else:
        out_specs = [pl.BlockSpec((seq_len, n_w), lambda s: (s, 0)),
                     pl.BlockSpec((None, 2, D_SSM, SSM_STATE), lambda s: (s, 0, 0, 0))]
        out_shape = [jax.ShapeDtypeStruct((T_CTX, n_w), BF16),
                     jax.ShapeDtypeStruct((n_seq, 2, D_SSM, SSM_STATE), F32)]
    return pl.pallas_call(
        functools.partial(_mixer_ab_kernel, seq_len, has_init),
        grid=(n_seq,),
        in_specs=in_specs,
        out_specs=out_specs,
        out_shape=out_shape,
        scratch_shapes=[
            pltpu.VMEM((seq_len, D_SSM), F32),
            pltpu.VMEM((seq_len, N_BC), F32),
            pltpu.VMEM((seq_len, LANES), F32),
            pltpu.VMEM((seq_len, D_SSM), F32),
            pltpu.VMEM((2, SSM_STATE, D_SSM), F32),
        ],
        compiler_params=_cparams(1),
        name="mixer_ab_lat" if has_init else "mixer_ab_ctx",
    )(*args)


ATT_SCALE = ATT_HEAD_DIM ** -0.5


def _dot_nt(a, b):
    return lax.dot_general(a, b, (((1,), (1,)), ((), ())), preferred_element_type=F32)


def _attn_ctx_kernel(qkv_ref, o_ref):
    for h in range(ATT_HEADS):
        hc = slice(h * ATT_HEAD_DIM, (h + 1) * ATT_HEAD_DIM)
        q = qkv_ref[:, h * ATT_HEAD_DIM:(h + 1) * ATT_HEAD_DIM]
        k = qkv_ref[:, D_ATT + h * ATT_HEAD_DIM:D_ATT + (h + 1) * ATT_HEAD_DIM]
        v = qkv_ref[:, 2 * D_ATT + h * ATT_HEAD_DIM:2 * D_ATT + (h + 1) * ATT_HEAD_DIM]
        s = _dot_nt(q, k) * ATT_SCALE
        p = jnp.exp(s - jnp.max(s, axis=-1, keepdims=True))
        l = jnp.sum(p, axis=-1, keepdims=True)
        o = jnp.dot(p.astype(BF16), v, preferred_element_type=F32)
        o_ref[:, hc] = (o / l).astype(BF16)


def _attn_ctx(pqkv):
    return pl.pallas_call(
        _attn_ctx_kernel,
        grid=(BATCH,),
        in_specs=[pl.BlockSpec((SEQ, W_QKV), lambda s: (s, 0))],
        out_specs=pl.BlockSpec((SEQ, D_ATT), lambda s: (s, 0)),
        out_shape=jax.ShapeDtypeStruct((T_CTX, D_ATT), BF16),
        compiler_params=_cparams(1),
        name="attn_ctx",
    )(pqkv)


def _na_win_start(i):
    return jnp.clip(2 * i - NA_ROWS // 2, 0, DEC_SEQ // GRID_W - NA_WIN_ROWS)


NA_BIAS_FLAT = 1024


def _attn_na_kernel(qkv_ref, kc_ref, vc_ref, rpb_ref, ctx_ref, both_ref, bias_ref):
    both_ref[0] = ctx_ref[...]
    o_ref = both_ref.at[1]
    i = pl.program_id(0)
    q0 = pl.multiple_of(i * NA_TQ, NA_TQ)
    w_row = _na_win_start(i)
    w0 = pl.multiple_of(w_row * GRID_W, 2 * GRID_W)

    @pl.when(pl.program_id(1) == 0)
    def _():
        ql = lax.broadcasted_iota(jnp.int32, (NA_TQ, NA_WIN), 0) + q0
        kl = lax.broadcasted_iota(jnp.int32, (NA_TQ, NA_WIN), 1) + w0
        qr, qc = ql // GRID_W, ql % GRID_W
        kr, kc = kl // GRID_W, kl % GRID_W
        rs = jnp.clip(qr - NA_ROWS // 2, 0, DEC_SEQ // GRID_W - NA_ROWS)
        cs = jnp.clip(qc - NA_COLS // 2, 0, GRID_W - NA_COLS)
        valid = (kr >= rs) & (kr < rs + NA_ROWS) & (kc >= cs) & (kc < cs + NA_COLS)
        centre = (NA_ROWS - 1) * GRID_W + NA_COLS - 1
        shift = lax.rem(NA_BIAS_FLAT - (w0 - q0 + centre), NA_BIAS_FLAT)
        for h in range(ATT_HEADS):
            table = jnp.broadcast_to(rpb_ref[h], (NA_TQ, NA_BIAS_FLAT))
            rolled = pltpu.roll(table, shift, 1, stride=1, stride_axis=0)
            bias_ref[h] = jnp.where(valid, rolled[:, :NA_WIN], NEG)

    for h in range(ATT_HEADS):
        hc = slice(h * ATT_HEAD_DIM, (h + 1) * ATT_HEAD_DIM)
        kcs = slice(D_ATT + h * ATT_HEAD_DIM, D_ATT + (h + 1) * ATT_HEAD_DIM)
        vcs = slice(2 * D_ATT + h * ATT_HEAD_DIM, 2 * D_ATT + (h + 1) * ATT_HEAD_DIM)
        q = qkv_ref[pl.ds(q0, NA_TQ), hc]
        k_loc = qkv_ref[pl.ds(w0, NA_WIN), kcs]
        v_loc = qkv_ref[pl.ds(w0, NA_WIN), vcs]
        k_ctx = kc_ref[:, hc].astype(BF16)
        v_ctx = vc_ref[:, hc].astype(BF16)
        s_loc = _dot_nt(q, k_loc) * ATT_SCALE + bias_ref[h]
        s_ctx = _dot_nt(q, k_ctx) * ATT_SCALE
        m = jnp.maximum(jnp.max(s_loc, axis=-1, keepdims=True), jnp.max(s_ctx, axis=-1, keepdims=True))
        p_loc = jnp.exp(s_loc - m)
        p_ctx = jnp.exp(s_ctx - m)
        l = jnp.sum(p_loc, axis=-1, keepdims=True) + jnp.sum(p_ctx, axis=-1, keepdims=True)
        o = (jnp.dot(p_loc.astype(BF16), v_loc, preferred_element_type=F32)
             + jnp.dot(p_ctx.astype(BF16), v_ctx, preferred_element_type=F32))
        o_ref[:, hc] = (o / l).astype(BF16)


def _attn_na(pqkv, cache_k4, cache_v4, rpb_flat, yc_ctx, layer):
    assert T_CTX == T_LAT
    lat0 = T_CTX // DEC_SEQ
    return pl.pallas_call(
        _attn_na_kernel,
        grid=(NA_TILES, DEC_BATCH),
        in_specs=[
            pl.BlockSpec((DEC_SEQ, W_QKV), lambda i, b: (lat0 + b, 0)),
            pl.BlockSpec((None, None, PAST_LEN, D_ATT), lambda i, b: (b, layer, 0, 0)),
            pl.BlockSpec((None, None, PAST_LEN, D_ATT), lambda i, b: (b, layer, 0, 0)),
            pl.BlockSpec((None, ATT_HEADS, 1, NA_BIAS_FLAT), lambda i, b: (layer, 0, 0, 0)),
            pl.BlockSpec((NA_TQ, D_ATT), lambda i, b: (b * NA_TILES + i, 0)),
        ],
        out_specs=pl.BlockSpec((2, NA_TQ, D_ATT), lambda i, b: (0, b * NA_TILES + i, 0)),
        out_shape=jax.ShapeDtypeStruct((2, T_CTX, D_ATT), BF16),
        scratch_shapes=[pltpu.VMEM((ATT_HEADS, NA_TQ, NA_WIN), F32)],
        compiler_params=_cparams(2),
        name="attn_na",
    )(pqkv, cache_k4, cache_v4, rpb_flat, yc_ctx)


TM_OUT = 512


R_E0, R_E1, R_W0, R_W1, R_P0, R_P1 = range(6)


def _outproj_kernel(with_router, *refs):
    if with_router:
        (yab_ref, yc_ref, x_ref, w_ref, g1_ref, nw_ref, sh_ref, sc_ref, rw_ref, rb_ref,
         x1_ref, h2_ref, route_ref, count_ref, base_ref) = refs
    else:
        yab_ref, yc_ref, x_ref, w_ref, g1_ref, nw_ref, sh_ref, sc_ref, x1_ref, h2_ref = refs
    n_ab = D_CONV + D_SSM
    out = (jnp.dot(yab_ref[...], w_ref[0:n_ab, :], preferred_element_type=F32)
           + jnp.dot(yc_ref[...], w_ref[n_ab:, :], preferred_element_type=F32))
    x1 = x_ref[...] + g1_ref[...] * out
    x1_ref[...] = x1
    h2 = _norm_mod(x1, nw_ref[...], sh_ref[...], sc_ref[...])
    h2_ref[...] = h2.astype(h2_ref.dtype)
    if with_router:
        @pl.when(pl.program_id(0) == 0)
        def _():
            base_ref[...] = jnp.zeros_like(base_ref)

        logits = jnp.dot(h2, rw_ref[...], preferred_element_type=F32, precision=lax.Precision.HIGHEST)
        logits = logits + rb_ref[...]
        lane = lax.broadcasted_iota(jnp.int32, logits.shape, 1)
        m1 = jnp.max(logits, axis=-1, keepdims=True)
        i1 = jnp.min(jnp.where(logits == m1, lane, LANES), axis=-1, keepdims=True)
        rest = jnp.where(lane == i1, -jnp.inf, logits)
        m2 = jnp.max(rest, axis=-1, keepdims=True)
        i2 = jnp.min(jnp.where(rest == m2, lane, LANES), axis=-1, keepdims=True)
        e2 = jnp.exp(m2 - m1)
        den = 1.0 + e2
        tm = logits.shape[0]
        sel0, sel1 = lane == i1, lane == i2
        picked = jnp.where(sel0 | sel1, 1.0, 0.0)
        earlier = (lax.broadcasted_iota(jnp.int32, (tm, tm), 1)
                   < lax.broadcasted_iota(jnp.int32, (tm, tm), 0)).astype(BF16)
        rank = base_ref[...] + jnp.dot(earlier, picked.astype(BF16), preferred_element_type=F32)
        p0 = jnp.sum(jnp.where(sel0, rank, 0.0), axis=-1, keepdims=True)
        p1 = jnp.sum(jnp.where(sel1, rank, 0.0), axis=-1, keepdims=True)
        base = base_ref[...] + jnp.sum(picked, axis=0, keepdims=True)
        base_ref[...] = base
        count_ref[...] = base
        rec = jnp.zeros(logits.shape, F32)
        for k, val in ((R_E0, i1.astype(F32)), (R_E1, i2.astype(F32)), (R_W0, 1.0 / den), (R_W1, e2 / den),
                       (R_P0, p0), (R_P1, p1)):
            rec = jnp.where(lane == k, val, rec)
        route_ref[...] = rec


def _outproj(yab, yc, x, mod4, w_out_b, norm2_w3, router, layer):
    tm = TM_OUT
    with_router = router is not None
    row = lambda w: pl.BlockSpec((tm, w), lambda i: (i, 0))
    in_specs = [
        row(D_CONV + D_SSM), row(D_ATT), row(D),
        pl.BlockSpec((None, D, D), lambda i: (layer, 0, 0)),
        _mod_spec(layer, 2, tm),
        pl.BlockSpec((None, 1, D), lambda i: (layer, 0, 0)),
        _mod_spec(layer, 3, tm),
        _mod_spec(layer, 4, tm),
    ]
    args = [yab, yc, x, w_out_b, mod4, norm2_w3, mod4, mod4]
    out_specs = [row(D), row(D)]
    out_shape = [jax.ShapeDtypeStruct((T_ALL, D), F32), jax.ShapeDtypeStruct((T_ALL, D), F32 if with_router else BF16)]
    scratch = []
    if with_router:
        rw, rb = router
        in_specs += [pl.BlockSpec(rw.shape, lambda i: (0, 0)), pl.BlockSpec(rb.shape, lambda i: (0, 0))]
        args += [rw, rb]
        out_specs += [row(LANES), pl.BlockSpec((1, LANES), lambda i: (0, 0))]
        out_shape += [jax.ShapeDtypeStruct((T_ALL, LANES), F32), jax.ShapeDtypeStruct((1, LANES), F32)]
        scratch = [pltpu.VMEM((1, LANES), F32)]
    return pl.pallas_call(
        functools.partial(_outproj_kernel, with_router),
        grid=(T_ALL // tm,),
        in_specs=in_specs,
        out_specs=out_specs,
        out_shape=out_shape,
        scratch_shapes=scratch,
        compiler_params=_cparams(1),
        name="outproj_router" if with_router else "outproj",
    )(*args)


TM_FF = 512


def _swiglu(h, w1, w3, w2):
    u = jnp.dot(h, w1, preferred_element_type=F32)
    v = jnp.dot(h, w3, preferred_element_type=F32)
    return jnp.dot((_silu(u) * v).astype(BF16), w2, preferred_element_type=F32)


def _ffn_kernel(h_ref, w1_ref, w3_ref, w2_ref, x_ref, g2_ref, o_ref, acc_ref):
    e = pl.program_id(1)

    @pl.when(e == 0)
    def _():
        acc_ref[...] = jnp.zeros_like(acc_ref)

    acc_ref[...] += _swiglu(h_ref[...], w1_ref[...], w3_ref[...], w2_ref[...])

    @pl.when(e == pl.num_programs(1) - 1)
    def _():
        o_ref[...] = x_ref[...] + g2_ref[...] * acc_ref[...]


def _ffn(h2, x1, mod4, w1, w3, w2, layer):
    tm = TM_FF
    n_chunks = w1.shape[0]
    row = lambda w: pl.BlockSpec((tm, w), lambda i, e: (i, 0))
    return pl.pallas_call(
        _ffn_kernel,
        grid=(T_ALL // tm, n_chunks),
        in_specs=[
            row(D),
            pl.BlockSpec((None, D, D_FF_EXPERT), lambda i, e: (e, 0, 0)),
            pl.BlockSpec((None, D, D_FF_EXPERT), lambda i, e: (e, 0, 0)),
            pl.BlockSpec((None, D_FF_EXPERT, D), lambda i, e: (e, 0, 0)),
            row(D),
            _mod_spec(layer, 5, tm),
        ],
        out_specs=row(D),
        out_shape=jax.ShapeDtypeStruct((T_ALL, D), F32),
        scratch_shapes=[pltpu.VMEM((tm, D), F32)],
        compiler_params=_cparams(2),
        name="ffn",
    )(h2, w1, w3, w2, x1, mod4)


TOP_K = 2
TM_X = 512
TM_R = 512
ROW_DMA_UNROLL = 8
N_SLOTS = TOP_K * T_ALL + N_EXPERTS * TM_X
N_XTILES = N_SLOTS // TM_X


def _route_plan(route, counts):
    cnt = counts[0, :N_EXPERTS].astype(jnp.int32)
    gsz = (cnt + TM_X - 1) // TM_X * TM_X
    ends = jnp.cumsum(gsz)
    offs = ends - gsz
    e = route[:, R_E0:R_E1 + 1].astype(jnp.int32)
    pos = route[:, R_P0:R_P1 + 1].astype(jnp.int32)
    sel = e[..., None] == jnp.arange(N_EXPERTS, dtype=jnp.int32)
    dest = jnp.sum(jnp.where(sel, offs, 0), axis=-1) + pos
    n_tiles = ends[-1] // TM_X
    tile = jnp.minimum(jnp.arange(N_XTILES, dtype=jnp.int32), n_tiles - 1)
    texp = jnp.sum((ends[None, :] <= (tile * TM_X)[:, None]).astype(jnp.int32), axis=-1)
    return (dest.reshape(T_ALL // TM_R, 1, TOP_K * TM_R).astype(jnp.int32), tile, texp.astype(jnp.int32),
            n_tiles.reshape(1).astype(jnp.int32))


def _row_copy(src, s, dst, d, sem):
    return pltpu.make_async_copy(src.at[pl.ds(s, 1)], dst.at[pl.ds(d, 1)], sem)


def _dispatch_kernel(dest_ref, h_ref, xs_in_ref, xs_ref, sem):
    del xs_in_ref

    def issue(t, carry):
        for k in range(TOP_K):
            _row_copy(h_ref, t, xs_ref, dest_ref[0, TOP_K * t + k], sem).start(priority=k)
        return carry

    lax.fori_loop(0, TM_R, issue, 0, unroll=ROW_DMA_UNROLL)

    def drain(t, carry):
        for k in range(TOP_K):
            _row_copy(h_ref, 0, xs_ref, 0, sem).wait()
        return carry

    lax.fori_loop(0, TM_R, drain, 0, unroll=ROW_DMA_UNROLL)


def _dispatch(dest, h2):
    xs0 = jnp.zeros((N_SLOTS, D), F32)
    return pl.pallas_call(
        _dispatch_kernel,
        grid=(T_ALL // TM_R,),
        in_specs=[
            pl.BlockSpec((None, 1, TOP_K * TM_R), lambda i: (i, 0, 0), memory_space=pltpu.SMEM),
            pl.BlockSpec((TM_R, D), lambda i: (i, 0)),
            pl.BlockSpec(memory_space=pl.ANY),
        ],
        out_specs=pl.BlockSpec(memory_space=pl.ANY),
        out_shape=jax.ShapeDtypeStruct((N_SLOTS, D), F32),
        scratch_shapes=[pltpu.SemaphoreType.DMA],
        input_output_aliases={2: 0},
        compiler_params=_cparams(1),
        name="moe_dispatch",
    )(dest, h2, xs0)


def _experts_kernel(tile_ref, texp_ref, nt_ref, xs_ref, w1_ref, w3_ref, w2_ref, ys_ref):
    del tile_ref, texp_ref
    live = pl.program_id(0) < nt_ref[0]

    @pl.when(live)
    def _():
        ys_ref[...] = _swiglu(xs_ref[...].astype(BF16), w1_ref[...], w3_ref[...], w2_ref[...])

    @pl.when(jnp.logical_not(live))
    def _():
        ys_ref[...] = jnp.zeros_like(ys_ref)


def _experts(tile, texp, n_tiles, xs, w1, w3, w2):
    wspec = lambda shape: pl.BlockSpec((None,) + shape, lambda j, tile, texp, nt: (texp[j], 0, 0))
    rows_in = pl.BlockSpec((TM_X, D), lambda j, tile, texp, nt: (tile[j], 0))
    rows_out = pl.BlockSpec((TM_X, D), lambda j, tile, texp, nt: (j, 0))
    return pl.pallas_call(
        _experts_kernel,
        grid_spec=pltpu.PrefetchScalarGridSpec(
            num_scalar_prefetch=3,
            grid=(N_XTILES,),
            in_specs=[rows_in, wspec((D, D_FF_EXPERT)), wspec((D, D_FF_EXPERT)), wspec((D_FF_EXPERT, D))],
            out_specs=rows_out,
        ),
        out_shape=jax.ShapeDtypeStruct((N_SLOTS, D), F32),
        compiler_params=_cparams(1),
        name="moe_experts",
    )(tile, texp, n_tiles, xs, w1, w3, w2)


def _combine_kernel(dest_ref, ys_ref, route_ref, x_ref, g2_ref, o_ref, y_ref, sem):
    def issue(t, carry):
        for k in range(TOP_K):
            _row_copy(ys_ref, dest_ref[0, TOP_K * t + k], y_ref.at[k], t, sem).start(priority=k)
        return carry

    lax.fori_loop(0, TM_R, issue, 0, unroll=ROW_DMA_UNROLL)

    def drain(t, carry):
        for k in range(TOP_K):
            _row_copy(ys_ref, 0, y_ref.at[k], 0, sem).wait()
        return carry

    lax.fori_loop(0, TM_R, drain, 0, unroll=ROW_DMA_UNROLL)
    route = route_ref[...]
    moe = route[:, R_W0:R_W0 + 1] * y_ref[0] + route[:, R_W1:R_W1 + 1] * y_ref[1]
    o_ref[...] = x_ref[...] + g2_ref[...] * moe


def _combine(dest, ys, route, x1, mod4, layer):
    row = lambda w: pl.BlockSpec((TM_R, w), lambda i: (i, 0))
    return pl.pallas_call(
        _combine_kernel,
        grid=(T_ALL // TM_R,),
        in_specs=[
            pl.BlockSpec((None, 1, TOP_K * TM_R), lambda i: (i, 0, 0), memory_space=pltpu.SMEM),
            pl.BlockSpec(memory_space=pl.ANY),
            row(LANES),
            row(D),
            _mod_spec(layer, 5, TM_R),
        ],
        out_specs=row(D),
        out_shape=jax.ShapeDtypeStruct((T_ALL, D), F32),
        scratch_shapes=[pltpu.VMEM((TOP_K, TM_R, D), F32), pltpu.SemaphoreType.DMA],
        compiler_params=_cparams(1),
        name="moe_combine",
    )(dest, ys, route, x1, mod4)


def _moe(h2, x1, route, counts, mod4, w1, w3, w2, layer):
    dest, tile, texp, n_tiles = _route_plan(route, counts)
    xs = _dispatch(dest, h2)
    ys = _experts(tile, texp, n_tiles, xs, w1, w3, w2)
    return _combine(dest, ys, route, x1, mod4, layer)


TM_NORM = 1024


def _final_norm_kernel(x_ref, w_ref, o_ref):
    x = x_ref[...]
    var = jnp.mean(x * x, axis=-1, keepdims=True)
    o_ref[...] = x * lax.rsqrt(var + EPS) * w_ref[...]


def _final_norm(x, w):
    return pl.pallas_call(
        _final_norm_kernel,
        grid=(T_ALL // TM_NORM,),
        in_specs=[pl.BlockSpec((TM_NORM, D), lambda i: (i, 0)), pl.BlockSpec((1, D), lambda i: (0, 0))],
        out_specs=pl.BlockSpec((TM_NORM, D), lambda i: (i, 0)),
        out_shape=jax.ShapeDtypeStruct((T_ALL, D), F32),
        compiler_params=_cparams(1),
        name="final_norm",
    )(x, w.reshape(1, D))


def _pad_lanes(a, fill=0.0):
    pad = [(0, 0)] * (a.ndim - 1) + [(0, LANES - a.shape[-1])]
    return jnp.pad(a, pad, constant_values=fill)


def _reorder_w_in(w_in):
    a_end = W_A
    s_end = a_end + W_S
    dt_end = s_end + 2 * SSM_HEADS
    w_dt = jnp.pad(w_in[:, :, s_end:dt_end], ((0, 0), (0, 0), (0, W_DT - 2 * SSM_HEADS)))
    return jnp.concatenate([w_in[:, :, :s_end], w_in[:, :, dt_end:], w_dt], axis=-1).astype(BF16)


def kernel(x_prompt, x_sample, cache_k, cache_v, state_ssm, c, c_ctx, norm1_w, norm2_w, w_ada, b_ada, w_in, w_out, conv_a_w, ssm_conv_w, ssm_conv_b, dt_bias, a_log, d_skip, ssm_norm_w, rpb, ffn_w1, ffn_w3, ffn_w2, router_w, router_b, moe_w1, moe_w3, moe_w2, final_norm_w):
    x = jnp.concatenate([x_prompt.reshape(T_CTX, D), x_sample.reshape(T_LAT, D)], axis=0)

    cvec = jnp.concatenate([c_ctx[None, :], c, jnp.zeros((8 - 1 - DEC_BATCH, D), F32)], axis=0)
    mod4 = _adaln_all(cvec, w_ada, b_ada).reshape(DEPTH, 8, 1, N_MOD * D)

    w_in_r = _reorder_w_in(w_in)
    w_out_b = w_out.astype(BF16)
    norm1_w3 = norm1_w.reshape(DEPTH, 1, D)
    norm2_w3 = norm2_w.reshape(DEPTH, 1, D)
    n_dense = ffn_w1.shape[0]
    split_cols = lambda w: jnp.transpose(w.reshape(n_dense, D, 2, D_FF_EXPERT), (0, 2, 1, 3)).astype(BF16)
    ffn_w1c, ffn_w3c = split_cols(ffn_w1), split_cols(ffn_w3)
    ffn_w2c = ffn_w2.reshape(n_dense, 2, D_FF_EXPERT, D).astype(BF16)
    moe_w1b, moe_w3b, moe_w2b = moe_w1.astype(BF16), moe_w3.astype(BF16), moe_w2.astype(BF16)
    router_wp = _pad_lanes(router_w)
    router_bp = _pad_lanes(router_b, NEG)
    dtb = _pad_lanes(dt_bias.reshape(DEPTH, 1, 2 * SSM_HEADS))
    alog = _pad_lanes(a_log.reshape(DEPTH, 1, 2 * SSM_HEADS))
    dsk = jnp.repeat(d_skip, SSM_HEAD_DIM, axis=-1).reshape(DEPTH, 1, D_SSM)
    rpb_flat = jnp.pad(rpb, ((0, 0), (0, 0), (0, 1), (0, GRID_W - rpb.shape[-1])))
    rpb_flat = rpb_flat.reshape(DEPTH, ATT_HEADS, 1, NA_BIAS_FLAT)
    cache_k4 = cache_k.reshape(DEC_BATCH, DEPTH, PAST_LEN, D_ATT)
    cache_v4 = cache_v.reshape(DEC_BATCH, DEPTH, PAST_LEN, D_ATT)
    state5 = state_ssm.reshape(DEC_BATCH, DEPTH, 2, D_SSM, SSM_STATE)

    ks, vs, sts = [], [], []
    for layer in range(DEPTH):
        pa, ps, pqkv, pdt = _inproj(x, mod4, norm1_w3, w_in_r, layer)
        params = (conv_a_w[layer], ssm_conv_w[layer], ssm_conv_b[layer].reshape(1, -1), dtb[layer], alog[layer],
                  dsk[layer], ssm_norm_w[layer].reshape(1, D_SSM))
        yab, st = _mixer_ab(pa, ps, pdt, None, params, SEQ, BATCH, 0, layer)
        yab = _mixer_ab(pa, ps, pdt, state5, params, DEC_SEQ, DEC_BATCH, T_CTX // DEC_SEQ, layer, yab)
        yab = yab.reshape(T_ALL, D_CONV + D_SSM)
        yc = _attn_na(pqkv, cache_k4, cache_v4, rpb_flat, _attn_ctx(pqkv), layer).reshape(T_ALL, D_ATT)
        ks.append(pqkv[:T_CTX, D_ATT:2 * D_ATT].astype(F32).reshape(BATCH, SEQ, ATT_HEADS, ATT_HEAD_DIM))
        vs.append(pqkv[:T_CTX, 2 * D_ATT:].astype(F32).reshape(BATCH, SEQ, ATT_HEADS, ATT_HEAD_DIM))
        sts.append(st.reshape(BATCH, 2, SSM_HEADS, SSM_HEAD_DIM, SSM_STATE))
        i = layer // 2
        if layer % 2 == 0:
            x1, h2 = _outproj(yab, yc, x, mod4, w_out_b, norm2_w3, None, layer)
            x = _ffn(h2, x1, mod4, ffn_w1c[i], ffn_w3c[i], ffn_w2c[i], layer)
        else:
            x1, h2, route, counts = _outproj(yab, yc, x, mod4, w_out_b, norm2_w3,
                                             (router_wp[i], router_bp[i][None, :]), layer)
            x = _moe(h2, x1, route, counts, mod4, moe_w1b[i], moe_w3b[i], moe_w2b[i], layer)

    y = _final_norm(x, final_norm_w)
    y_prompt = y[:T_CTX].reshape(BATCH, SEQ, D)
    y_sample = y[T_CTX:].reshape(DEC_BATCH, DEC_SEQ, D)
    return (y_prompt, y_sample, jnp.stack(ks, axis=1), jnp.stack(vs, axis=1), jnp.stack(sts, axis=1))
```

```python
import functools

import jax
import jax.numpy as jnp
from jax import lax
from jax.experimental import pallas as pl
from jax.experimental.pallas import tpu as pltpu

F32 = jnp.float32
BF16 = jnp.bfloat16

D = 1024
BATCH = 16
SEQ = 256
DEPTH = 4
DEC_BATCH = 4
DEC_SEQ = 1024
PAST_LEN = 512
GRID_W = 64
D_CONV = 256
D_SSM = 384
SSM_HEADS = 6
SSM_HEAD_DIM = 64
SSM_STATE = 128
CHUNK = 128
N_BC = 512
D_ATT = 384
ATT_HEADS = 6
ATT_HEAD_DIM = 64
NA_ROWS = 8
NA_COLS = 16
D_FF = 2816
N_EXPERTS = 8
D_FF_EXPERT = 1408
N_MOD = 6
EPS = 1e-6

T_CTX = BATCH * SEQ
T_LAT = DEC_BATCH * DEC_SEQ
T_ALL = T_CTX + T_LAT

W_A = 3 * D_CONV
W_S = 2 * D_SSM + N_BC
W_QKV = 3 * D_ATT
W_DT = 128
N_PROJ_PAD = W_A + W_S + W_QKV + W_DT

LANES = 128
NEG = -1e30

NA_TQ = 128
NA_TILES = DEC_SEQ // NA_TQ
NA_WIN_ROWS = 10
NA_WIN = NA_WIN_ROWS * GRID_W

VMEM_LIMIT = 56 * 1024 * 1024


def _cparams(n_axes):
    return pltpu.CompilerParams(dimension_semantics=("arbitrary",) * n_axes, vmem_limit_bytes=VMEM_LIMIT)


def _silu(x):
    return x * jax.nn.sigmoid(x)


def _mod_row(i, tm):
    n_ctx = T_CTX // tm
    per_lat = DEC_SEQ // tm
    return jnp.where(i < n_ctx, 0, (i - n_ctx) // per_lat + 1)


def _mod_spec(layer, k, tm):
    return pl.BlockSpec((None, None, 1, D), lambda i, *_: (layer, _mod_row(i, tm), 0, k))


ADA_TN = 1536


def _adaln_kernel(c_ref, w_ref, b_ref, o_ref):
    s = _silu(c_ref[...]).astype(BF16)
    o_ref[...] = jnp.dot(s, w_ref[...].astype(BF16), preferred_element_type=F32) + b_ref[...]


def _adaln_all(cvec, w_ada, b_ada):
    n = N_MOD * D
    return pl.pallas_call(
        _adaln_kernel,
        grid=(DEPTH, n // ADA_TN),
        in_specs=[
            pl.BlockSpec((8, D), lambda l, j: (0, 0)),
            pl.BlockSpec((None, D, ADA_TN), lambda l, j: (l, 0, j)),
            pl.BlockSpec((None, 1, ADA_TN), lambda l, j: (l, 0, j)),
        ],
        out_specs=pl.BlockSpec((None, 8, ADA_TN), lambda l, j: (l, 0, j)),
        out_shape=jax.ShapeDtypeStruct((DEPTH, 8, n), F32),
        compiler_params=_cparams(2),
        name="adaln",
    )(cvec, w_ada, b_ada.reshape(DEPTH, 1, n))


TM_IN = 512


def _norm_mod(x, nw, sh, sc):
    var = jnp.mean(x * x, axis=-1, keepdims=True)
    h = x * lax.rsqrt(var + EPS) * nw
    return h * (1.0 + sc) + sh


def _inproj_kernel(x_ref, nw_ref, sh_ref, sc_ref, w_ref, pa_ref, ps_ref, pqkv_ref, pdt_ref):
    hb = _norm_mod(x_ref[...], nw_ref[...], sh_ref[...], sc_ref[...]).astype(BF16)
    o = 0
    pa_ref[...] = jnp.dot(hb, w_ref[:, o:o + W_A], preferred_element_type=F32).astype(BF16)
    o += W_A
    ps_ref[...] = jnp.dot(hb, w_ref[:, o:o + W_S], preferred_element_type=F32).astype(BF16)
    o += W_S
    pqkv_ref[...] = jnp.dot(hb, w_ref[:, o:o + W_QKV], preferred_element_type=F32).astype(BF16)
    o += W_QKV
    pdt_ref[...] = jnp.dot(hb, w_ref[:, o:o + W_DT], preferred_element_type=F32)


def _inproj(x, mod4, norm1_w3, w_in_r, layer):
    tm = TM_IN
    row = lambda w: pl.BlockSpec((tm, w), lambda i: (i, 0))
    return pl.pallas_call(
        _inproj_kernel,
        grid=(T_ALL // tm,),
        in_specs=[
            row(D),
            pl.BlockSpec((None, 1, D), lambda i: (layer, 0, 0)),
            _mod_spec(layer, 0, tm),
            _mod_spec(layer, 1, tm),
            pl.BlockSpec((None, D, N_PROJ_PAD), lambda i: (layer, 0, 0)),
        ],
        out_specs=[row(W_A), row(W_S), row(W_QKV), row(W_DT)],
        out_shape=[
            jax.ShapeDtypeStruct((T_ALL, W_A), BF16),
            jax.ShapeDtypeStruct((T_ALL, W_S), BF16),
            jax.ShapeDtypeStruct((T_ALL, W_QKV), BF16),
            jax.ShapeDtypeStruct((T_ALL, W_DT), F32),
        ],
        compiler_params=_cparams(1),
        name="inproj",
    )(x, norm1_w3, mod4, mod4, w_in_r)


HALO = 16


def _conv3(cur, prev_row, next_row, w):
    n = cur.shape[0]
    rid = lax.broadcasted_iota(jnp.int32, cur.shape, 0)
    dn = jnp.where(rid == 0, prev_row, pltpu.roll(cur, 1, 0))
    up = jnp.where(rid == n - 1, next_row, pltpu.roll(cur, n - 1, 0))
    return w[0:1] * dn + w[1:2] * cur + w[2:3] * up


def _cumsum_rows(x):
    rid = lax.broadcasted_iota(jnp.int32, x.shape, 0)
    s = 1
    while s < x.shape[0]:
        x = x + jnp.where(rid >= s, pltpu.roll(x, s, 0), 0.0)
        s *= 2
    return x


def _spread_exact(x, onehot):
    hi = x.astype(BF16)
    rest = x - hi.astype(F32)
    mid = rest.astype(BF16)
    lo = (rest - mid.astype(F32)).astype(BF16)
    return (jnp.dot(hi, onehot, preferred_element_type=F32) + jnp.dot(mid, onehot, preferred_element_type=F32)
            + jnp.dot(lo, onehot, preferred_element_type=F32))


def _mixer_ab_kernel(seq_len, has_init, *refs):
    if has_init:
        (pa_ref, ps_ref, pdt_ref, init_ref, wa_ref, cw_ref, cb_ref, dtb_ref, alog_ref, dsk_ref, nw_ref, ctx_ref,
         both_ref, xs_ref, bc_ref, dt_ref, y_ref, rt_ref) = refs
        st_ref = None
        both_ref[0] = ctx_ref[...]
        yab_ref = both_ref.at[1]
    else:
        (pa_ref, ps_ref, pdt_ref, wa_ref, cw_ref, cb_ref, dtb_ref, alog_ref, dsk_ref, nw_ref,
         yab_ref, st_ref, xs_ref, bc_ref, dt_ref, y_ref, rt_ref) = refs
        init_ref = None
    nc = seq_len // CHUNK

    def halo(ref, r0, c):
        lo = jnp.maximum(r0 - HALO, 0)
        hi = jnp.minimum(r0 + CHUNK, seq_len - HALO)
        prev = ref[pl.ds(pl.multiple_of(lo, HALO), HALO), :].astype(F32)[HALO - 1:HALO]
        nxt = ref[pl.ds(pl.multiple_of(hi, HALO), HALO), :].astype(F32)[0:1]
        return jnp.where(c > 0, prev, 0.0), jnp.where(c < nc - 1, nxt, 0.0)

    def prep(c, carry):
        r0 = pl.multiple_of(c * CHUNK, CHUNK)
        rows = pl.ds(r0, CHUNK)
        pa = pa_ref[rows, :].astype(F32)
        pp, pn = halo(pa_ref, r0, c)
        g = pa[:, D_CONV:2 * D_CONV] * pa[:, 2 * D_CONV:]
        gp = pp[:, D_CONV:2 * D_CONV] * pp[:, 2 * D_CONV:]
        gn = pn[:, D_CONV:2 * D_CONV] * pn[:, 2 * D_CONV:]
        ya = pa[:, :D_CONV] * _conv3(g, gp, gn, wa_ref[...])
        yab_ref[rows, 0:D_CONV] = ya.astype(BF16)

        ps = ps_ref[rows, :].astype(F32)
        sp, sn = halo(ps_ref, r0, c)
        cw = cw_ref[...]
        cb = cb_ref[...]
        xs = _conv3(ps[:, :D_SSM], sp[:, :D_SSM], sn[:, :D_SSM], cw[:, :D_SSM]) + cb[:, :D_SSM]
        xs_ref[rows, :] = _silu(xs)
        o = 2 * D_SSM
        bc = _conv3(ps[:, o:], sp[:, o:], sn[:, o:], cw[:, D_SSM:]) + cb[:, D_SSM:]
        bc_ref[rows, :] = _silu(bc)
        t = pdt_ref[rows, :] + dtb_ref[...]
        dt_ref[rows, :] = jnp.maximum(t, 0.0) + jnp.log1p(jnp.exp(-jnp.abs(t)))
        return carry

    lax.fori_loop(0, nc, prep, 0)

    for d in range(2):
        for j in range(D_SSM // LANES):
            cols = slice(j * LANES, (j + 1) * LANES)
            if has_init:
                rt_ref[d, :, cols] = init_ref[d, cols, :].T
            else:
                rt_ref[d, :, cols] = jnp.zeros((SSM_STATE, LANES), F32)

    a_row = -jnp.exp(alog_ref[...])

    def ssd_chunk(c, d):
        rows = pl.ds(pl.multiple_of(c * CHUNK, CHUNK), CHUNK)
        xs = xs_ref[rows, :]
        bc = bc_ref[rows, :]
        dt = dt_ref[rows, :]
        adt = dt * a_row
        cs = _cumsum_rows(adt)
        total = cs[CHUNK - 1:CHUNK, :]
        e = cs if d == 0 else total - cs + adt
        e_t = e.T
        dec = jnp.exp(total - e)
        ee = jnp.exp(e)
        lane_k = lax.broadcasted_iota(jnp.int32, (LANES, D_SSM), 0)
        to_cols = jnp.where(lane_k == d * SSM_HEADS + lax.broadcasted_iota(jnp.int32, (LANES, D_SSM), 1)
                            // SSM_HEAD_DIM, 1.0, 0.0).astype(BF16)
        lane_k2 = lax.broadcasted_iota(jnp.int32, (LANES, SSM_HEADS * CHUNK), 0)
        to_blocks = jnp.where(lane_k2 == d * SSM_HEADS
                              + lax.broadcasted_iota(jnp.int32, (LANES, SSM_HEADS * CHUNK), 1) // CHUNK,
                              1.0, 0.0).astype(BF16)
        spread = _spread_exact(jnp.concatenate([dt, ee, dec], axis=0), to_cols)
        dt_x, ee_x, dec_x = spread[:CHUNK], spread[CHUNK:2 * CHUNK], spread[2 * CHUNK:]
        etot_x = ee_x[CHUNK - 1:CHUNK] if d == 0 else ee_x[0:1]
        e_blk = _spread_exact(e, to_blocks)
        xdt = xs * dt_x
        xdt_b = xdt.astype(BF16)
        xdd_b = (xdt * dec_x).astype(BF16)
        rt = rt_ref[d]
        rt_b = rt.astype(BF16)
        ri = lax.broadcasted_iota(jnp.int32, (CHUNK, CHUNK), 0)
        ci = lax.broadcasted_iota(jnp.int32, (CHUNK, CHUNK), 1)
        valid = (ri >= ci) if d == 0 else (ri <= ci)
        col_head = lax.broadcasted_iota(jnp.int32, (CHUNK, D_SSM), 1) // SSM_HEAD_DIM
        heads_per_group = SSM_HEADS // 2
        heads_per_tile = LANES // SSM_HEAD_DIM
        tile_head = lax.broadcasted_iota(jnp.int32, (CHUNK, LANES), 1) // SSM_HEAD_DIM
        y_tiles = [jnp.zeros((CHUNK, LANES), F32) for _ in range(D_SSM // LANES)]
        y_off = jnp.zeros((CHUNK, D_SSM), F32)
        st = jnp.zeros((SSM_STATE, D_SSM), F32)
        for g in range(2):
            in_group = col_head // heads_per_group == g
            bg = bc[:, g * SSM_STATE:(g + 1) * SSM_STATE]
            cg = bc[:, 2 * SSM_STATE + g * SSM_STATE:2 * SSM_STATE + (g + 1) * SSM_STATE].astype(BF16)
            cb = lax.dot_general(cg, bg.astype(BF16), (((1,), (1,)), ((), ())), preferred_element_type=F32)
            bg_t = bg.T.astype(BF16)
            y_off = y_off + jnp.dot(cg, jnp.where(in_group, rt_b, jnp.zeros_like(rt_b)),
                                    preferred_element_type=F32)
            st = jnp.where(in_group, jnp.dot(bg_t, xdd_b, preferred_element_type=F32), st)
            for hh in range(heads_per_group):
                h = g * heads_per_group + hh
                k = d * SSM_HEADS + h
                seg = e_blk[:, h * CHUNK:(h + 1) * CHUNK] - e_t[k:k + 1, :]
                lm = jnp.where(valid, jnp.exp(jnp.where(valid, seg, 0.0)), 0.0)
                j = h // heads_per_tile
                x_tile = xdt_b[:, j * LANES:(j + 1) * LANES]
                x_tile = jnp.where(tile_head == h % heads_per_tile, x_tile, jnp.zeros_like(x_tile))
                y_tiles[j] = y_tiles[j] + jnp.dot((cb * lm).astype(BF16), x_tile, preferred_element_type=F32)
        y = jnp.concatenate(y_tiles, axis=1) + y_off * ee_x
        if d == 0:
            y_ref[rows, :] = y
        else:
            y_ref[rows, :] = y_ref[rows, :] + y
        rt_ref[d] = rt * etot_x + st

    lax.fori_loop(0, nc, lambda i, cr: (ssd_chunk(i, 0), cr)[1], 0)
    lax.fori_loop(0, nc, lambda i, cr: (ssd_chunk(nc - 1 - i, 1), cr)[1], 0)

    if not has_init:
        for d in range(2):
            for j in range(D_SSM // LANES):
                cols = slice(j * LANES, (j + 1) * LANES)
                st_ref[d, cols, :] = rt_ref[d, :, cols].T

    def finish(c, carry):
        rows = pl.ds(pl.multiple_of(c * CHUNK, CHUNK), CHUNK)
        y = y_ref[rows, :] + dsk_ref[...] * xs_ref[rows, :]
        y = y * _silu(ps_ref[rows, D_SSM:2 * D_SSM].astype(F32))
        col = lax.broadcasted_iota(jnp.int32, y.shape, 1)
        first = col < D_SSM // 2
        ysq = y * y
        s0 = jnp.sum(jnp.where(first, ysq, 0.0), axis=-1, keepdims=True)
        s1 = jnp.sum(jnp.where(first, 0.0, ysq), axis=-1, keepdims=True)
        inv = 1.0 / (D_SSM // 2)
        r = jnp.where(first, lax.rsqrt(s0 * inv + EPS), lax.rsqrt(s1 * inv + EPS))
        yab_ref[rows, D_CONV:] = (y * r * nw_ref[...]).astype(BF16)
        return carry

    lax.fori_loop(0, nc, finish, 0)


def _mixer_ab(pa, ps, pdt, init, params, seq_len, n_seq, blk0, layer, yab=None):
    has_init = init is not None
    n_w = D_CONV + D_SSM
    seq = lambda w: pl.BlockSpec((seq_len, w), lambda s: (blk0 + s, 0))
    full = lambda a: pl.BlockSpec(a.shape, lambda s: (0,) * a.ndim)
    in_specs = [seq(W_A), seq(W_S), seq(W_DT)]
    args = [pa, ps, pdt]
    if has_init:
        in_specs.append(pl.BlockSpec((None, None, 2, D_SSM, SSM_STATE), lambda s: (s, layer, 0, 0, 0)))
        args.append(init)
    in_specs += [full(p) for p in params]
    args += list(params)
    if has_init:
        assert T_CTX == T_LAT
        in_specs.append(pl.BlockSpec((seq_len, n_w), lambda s: (s, 0)))
        args.append(yab)
        out_specs = pl.BlockSpec((2, seq_len, n_w), lambda s: (0, s, 0))
        out_shape = jax.ShapeDtypeStruct((2, T_CTX, n_w), BF16)
    else:
        out_specs = [pl.BlockSpec((seq_len, n_w), lambda s: (s, 0)),
                     pl.BlockSpec((None, 2, D_SSM, SSM_STATE), lambda s: (s, 0, 0, 0))]
        out_shape = [jax.ShapeDtypeStruct((T_CTX, n_w), BF16),
                     jax.ShapeDtypeStruct((n_seq, 2, D_SSM, SSM_STATE), F32)]
    return pl.pallas_call(
        functools.partial(_mixer_ab_kernel, seq_len, has_init),
        grid=(n_seq,),
        in_specs=in_specs,
        out_specs=out_specs,
        out_shape=out_shape,
        scratch_shapes=[
            pltpu.VMEM((seq_len, D_SSM), F32),
            pltpu.VMEM((seq_len, N_BC), F32),
            pltpu.VMEM((seq_len, LANES), F32),
            pltpu.VMEM((seq_len, D_SSM), F32),
            pltpu.VMEM((2, SSM_STATE, D_SSM), F32),
        ],
        compiler_params=_cparams(1),
        name="mixer_ab_lat" if has_init else "mixer_ab_ctx",
    )(*args)


ATT_SCALE = ATT_HEAD_DIM ** -0.5


def _dot_nt(a, b):
    return lax.dot_general(a, b, (((1,), (1,)), ((), ())), preferred_element_type=F32)


def _attn_ctx_kernel(qkv_ref, o_ref):
    for h in range(ATT_HEADS):
        hc = slice(h * ATT_HEAD_DIM, (h + 1) * ATT_HEAD_DIM)
        q = qkv_ref[:, h * ATT_HEAD_DIM:(h + 1) * ATT_HEAD_DIM]
        k = qkv_ref[:, D_ATT + h * ATT_HEAD_DIM:D_ATT + (h + 1) * ATT_HEAD_DIM]
        v = qkv_ref[:, 2 * D_ATT + h * ATT_HEAD_DIM:2 * D_ATT + (h + 1) * ATT_HEAD_DIM]
        s = _dot_nt(q, k) * ATT_SCALE
        p = jnp.exp(s - jnp.max(s, axis=-1, keepdims=True))
        l = jnp.sum(p, axis=-1, keepdims=True)
        o = jnp.dot(p.astype(BF16), v, preferred_element_type=F32)
        o_ref[:, hc] = (o / l).astype(BF16)


def _attn_ctx(pqkv):
    return pl.pallas_call(
        _attn_ctx_kernel,
        grid=(BATCH,),
        in_specs=[pl.BlockSpec((SEQ, W_QKV), lambda s: (s, 0))],
        out_specs=pl.BlockSpec((SEQ, D_ATT), lambda s: (s, 0)),
        out_shape=jax.ShapeDtypeStruct((T_CTX, D_ATT), BF16),
        compiler_params=_cparams(1),
        name="attn_ctx",
    )(pqkv)


def _na_win_start(i):
    return jnp.clip(2 * i - NA_ROWS // 2, 0, DEC_SEQ // GRID_W - NA_WIN_ROWS)


NA_BIAS_FLAT = 1024


def _attn_na_kernel(qkv_ref, kc_ref, vc_ref, rpb_ref, ctx_ref, both_ref, bias_ref):
    both_ref[0] = ctx_ref[...]
    o_ref = both_ref.at[1]
    i = pl.program_id(0)
    q0 = pl.multiple_of(i * NA_TQ, NA_TQ)
    w_row = _na_win_start(i)
    w0 = pl.multiple_of(w_row * GRID_W, 2 * GRID_W)

    @pl.when(pl.program_id(1) == 0)
    def _():
        ql = lax.broadcasted_iota(jnp.int32, (NA_TQ, NA_WIN), 0) + q0
        kl = lax.broadcasted_iota(jnp.int32, (NA_TQ, NA_WIN), 1) + w0
        qr, qc = ql // GRID_W, ql % GRID_W
        kr, kc = kl // GRID_W, kl % GRID_W
        rs = jnp.clip(qr - NA_ROWS // 2, 0, DEC_SEQ // GRID_W - NA_ROWS)
        cs = jnp.clip(qc - NA_COLS // 2, 0, GRID_W - NA_COLS)
        valid = (kr >= rs) & (kr < rs + NA_ROWS) & (kc >= cs) & (kc < cs + NA_COLS)
        centre = (NA_ROWS - 1) * GRID_W + NA_COLS - 1
        shift = lax.rem(NA_BIAS_FLAT - (w0 - q0 + centre), NA_BIAS_FLAT)
        for h in range(ATT_HEADS):
            table = jnp.broadcast_to(rpb_ref[h], (NA_TQ, NA_BIAS_FLAT))
            rolled = pltpu.roll(table, shift, 1, stride=1, stride_axis=0)
            bias_ref[h] = jnp.where(valid, rolled[:, :NA_WIN], NEG)

    for h in range(ATT_HEADS):
        hc = slice(h * ATT_HEAD_DIM, (h + 1) * ATT_HEAD_DIM)
        kcs = slice(D_ATT + h * ATT_HEAD_DIM, D_ATT + (h + 1) * ATT_HEAD_DIM)
        vcs = slice(2 * D_ATT + h * ATT_HEAD_DIM, 2 * D_ATT + (h + 1) * ATT_HEAD_DIM)
        q = qkv_ref[pl.ds(q0, NA_TQ), hc]
        k_loc = qkv_ref[pl.ds(w0, NA_WIN), kcs]
        v_loc = qkv_ref[pl.ds(w0, NA_WIN), vcs]
        k_ctx = kc_ref[:, hc].astype(BF16)
        v_ctx = vc_ref[:, hc].astype(BF16)
        s_loc = _dot_nt(q, k_loc) * ATT_SCALE + bias_ref[h]
        s_ctx = _dot_nt(q, k_ctx) * ATT_SCALE
        m = jnp.maximum(jnp.max(s_loc, axis=-1, keepdims=True), jnp.max(s_ctx, axis=-1, keepdims=True))
        p_loc = jnp.exp(s_loc - m)
        p_ctx = jnp.exp(s_ctx - m)
        l = jnp.sum(p_loc, axis=-1, keepdims=True) + jnp.sum(p_ctx, axis=-1, keepdims=True)
        o = (jnp.dot(p_loc.astype(BF16), v_loc, preferred_element_type=F32)
             + jnp.dot(p_ctx.astype(BF16), v_ctx, preferred_element_type=F32))
        o_ref[:, hc] = (o / l).astype(BF16)


def _attn_na(pqkv, cache_k4, cache_v4, rpb_flat, yc_ctx, layer):
    assert T_CTX == T_LAT
    lat0 = T_CTX // DEC_SEQ
    return pl.pallas_call(
        _attn_na_kernel,
        grid=(NA_TILES, DEC_BATCH),
        in_specs=[
            pl.BlockSpec((DEC_SEQ, W_QKV), lambda i, b: (lat0 + b, 0)),
            pl.BlockSpec((None, None, PAST_LEN, D_ATT), lambda i, b: (b, layer, 0, 0)),
            pl.BlockSpec((None, None, PAST_LEN, D_ATT), lambda i, b: (b, layer, 0, 0)),
            pl.BlockSpec((None, ATT_HEADS, 1, NA_BIAS_FLAT), lambda i, b: (layer, 0, 0, 0)),
            pl.BlockSpec((NA_TQ, D_ATT), lambda i, b: (b * NA_TILES + i, 0)),
        ],
        out_specs=pl.BlockSpec((2, NA_TQ, D_ATT), lambda i, b: (0, b * NA_TILES + i, 0)),
        out_shape=jax.ShapeDtypeStruct((2, T_CTX, D_ATT), BF16),
        scratch_shapes=[pltpu.VMEM((ATT_HEADS, NA_TQ, NA_WIN), F32)],
        compiler_params=_cparams(2),
        name="attn_na",
    )(pqkv, cache_k4, cache_v4, rpb_flat, yc_ctx)


TM_OUT = 512


R_E0, R_E1, R_W0, R_W1, R_P0, R_P1 = range(6)


def _outproj_kernel(with_router, *refs):
    if with_router:
        (yab_ref, yc_ref, x_ref, w_ref, g1_ref, nw_ref, sh_ref, sc_ref, rw_ref, rb_ref,
         x1_ref, h2_ref, route_ref, count_ref, base_ref) = refs
    else:
        yab_ref, yc_ref, x_ref, w_ref, g1_ref, nw_ref, sh_ref, sc_ref, x1_ref, h2_ref = refs
    n_ab = D_CONV + D_SSM
    out = (jnp.dot(yab_ref[...], w_ref[0:n_ab, :], preferred_element_type=F32)
           + jnp.dot(yc_ref[...], w_ref[n_ab:, :], preferred_element_type=F32))
    x1 = x_ref[...] + g1_ref[...] * out
    x1_ref[...] = x1
    h2 = _norm_mod(x1, nw_ref[...], sh_ref[...], sc_ref[...])
    h2_ref[...] = h2.astype(h2_ref.dtype)
    if with_router:
        @pl.when(pl.program_id(0) == 0)
        def _():
            base_ref[...] = jnp.zeros_like(base_ref)

        rw = rw_ref[...]
        h_hi, w_hi = h2.astype(BF16), rw.astype(BF16)
        h_lo, w_lo = (h2 - h_hi.astype(F32)).astype(BF16), (rw - w_hi.astype(F32)).astype(BF16)
        logits = (jnp.dot(h_hi, w_hi, preferred_element_type=F32) + jnp.dot(h_hi, w_lo, preferred_element_type=F32)
                  + jnp.dot(h_lo, w_hi, preferred_element_type=F32))
        logits = logits + rb_ref[...]
        lane = lax.broadcasted_iota(jnp.int32, logits.shape, 1)
        m1 = jnp.max(logits, axis=-1, keepdims=True)
        i1 = jnp.min(jnp.where(logits == m1, lane, LANES), axis=-1, keepdims=True)
        rest = jnp.where(lane == i1, -jnp.inf, logits)
        m2 = jnp.max(rest, axis=-1, keepdims=True)
        i2 = jnp.min(jnp.where(rest == m2, lane, LANES), axis=-1, keepdims=True)
        e2 = jnp.exp(m2 - m1)
        den = 1.0 + e2
        tm = logits.shape[0]
        sel0, sel1 = lane == i1, lane == i2
        picked = jnp.where(sel0 | sel1, 1.0, 0.0)
        earlier = (lax.broadcasted_iota(jnp.int32, (tm, tm), 1)
                   < lax.broadcasted_iota(jnp.int32, (tm, tm), 0)).astype(BF16)
        rank = base_ref[...] + jnp.dot(earlier, picked.astype(BF16), preferred_element_type=F32)
        p0 = jnp.sum(jnp.where(sel0, rank, 0.0), axis=-1, keepdims=True)
        p1 = jnp.sum(jnp.where(sel1, rank, 0.0), axis=-1, keepdims=True)
        base = base_ref[...] + jnp.sum(picked, axis=0, keepdims=True)
        base_ref[...] = base
        count_ref[...] = base
        rec = jnp.zeros(logits.shape, F32)
        for k, val in ((R_E0, i1.astype(F32)), (R_E1, i2.astype(F32)), (R_W0, 1.0 / den), (R_W1, e2 / den),
                       (R_P0, p0), (R_P1, p1)):
            rec = jnp.where(lane == k, val, rec)
        route_ref[...] = rec


def _outproj(yab, yc, x, mod4, w_out_b, norm2_w3, router, layer):
    tm = TM_OUT
    with_router = router is not None
    row = lambda w: pl.BlockSpec((tm, w), lambda i: (i, 0))
    in_specs = [
        row(D_CONV + D_SSM), row(D_ATT), row(D),
        pl.BlockSpec((None, D, D), lambda i: (layer, 0, 0)),
        _mod_spec(layer, 2, tm),
        pl.BlockSpec((None, 1, D), lambda i: (layer, 0, 0)),
        _mod_spec(layer, 3, tm),
        _mod_spec(layer, 4, tm),
    ]
    args = [yab, yc, x, w_out_b, mod4, norm2_w3, mod4, mod4]
    out_specs = [row(D), row(D)]
    out_shape = [jax.ShapeDtypeStruct((T_ALL, D), F32), jax.ShapeDtypeStruct((T_ALL, D), F32 if with_router else BF16)]
    scratch = []
    if with_router:
        rw, rb = router
        in_specs += [pl.BlockSpec(rw.shape, lambda i: (0, 0)), pl.BlockSpec(rb.shape, lambda i: (0, 0))]
        args += [rw, rb]
        out_specs += [row(LANES), pl.BlockSpec((1, LANES), lambda i: (0, 0))]
        out_shape += [jax.ShapeDtypeStruct((T_ALL, LANES), F32), jax.ShapeDtypeStruct((1, LANES), F32)]
        scratch = [pltpu.VMEM((1, LANES), F32)]
    return pl.pallas_call(
        functools.partial(_outproj_kernel, with_router),
        grid=(T_ALL // tm,),
        in_specs=in_specs,
        out_specs=out_specs,
        out_shape=out_shape,
        scratch_shapes=scratch,
        compiler_params=_cparams(1),
        name="outproj_router" if with_router else "outproj",
    )(*args)


TM_FF = 512


def _swiglu(h, w1, w3, w2):
    u = jnp.dot(h, w1, preferred_element_type=F32)
    v = jnp.dot(h, w3, preferred_element_type=F32)
    return jnp.dot((_silu(u) * v).astype(BF16), w2, preferred_element_type=F32)


def _ffn_kernel(h_ref, w1_ref, w3_ref, w2_ref, x_ref, g2_ref, o_ref, acc_ref):
    e = pl.program_id(1)

    @pl.when(e == 0)
    def _():
        acc_ref[...] = jnp.zeros_like(acc_ref)

    acc_ref[...] += _swiglu(h_ref[...], w1_ref[...], w3_ref[...], w2_ref[...])

    @pl.when(e == pl.num_programs(1) - 1)
    def _():
        o_ref[...] = x_ref[...] + g2_ref[...] * acc_ref[...]


def _ffn(h2, x1, mod4, w1, w3, w2, layer):
    tm = TM_FF
    n_chunks = w1.shape[0]
    row = lambda w: pl.BlockSpec((tm, w), lambda i, e: (i, 0))
    return pl.pallas_call(
        _ffn_kernel,
        grid=(T_ALL // tm, n_chunks),
        in_specs=[
            row(D),
            pl.BlockSpec((None, D, D_FF_EXPERT), lambda i, e: (e, 0, 0)),
            pl.BlockSpec((None, D, D_FF_EXPERT), lambda i, e: (e, 0, 0)),
            pl.BlockSpec((None, D_FF_EXPERT, D), lambda i, e: (e, 0, 0)),
            row(D),
            _mod_spec(layer, 5, tm),
        ],
        out_specs=row(D),
        out_shape=jax.ShapeDtypeStruct((T_ALL, D), F32),
        scratch_shapes=[pltpu.VMEM((tm, D), F32)],
        compiler_params=_cparams(2),
        name="ffn",
    )(h2, w1, w3, w2, x1, mod4)


TOP_K = 2
TM_X = 512
TM_R = 512
ROW_DMA_UNROLL = 8
N_SLOTS = TOP_K * T_ALL + N_EXPERTS * TM_X
N_XTILES = N_SLOTS // TM_X


def _route_plan(route, counts):
    cnt = counts[0, :N_EXPERTS].astype(jnp.int32)
    gsz = (cnt + TM_X - 1) // TM_X * TM_X
    ends = jnp.cumsum(gsz)
    offs = ends - gsz
    e = route[:, R_E0:R_E1 + 1].astype(jnp.int32)
    pos = route[:, R_P0:R_P1 + 1].astype(jnp.int32)
    sel = e[..., None] == jnp.arange(N_EXPERTS, dtype=jnp.int32)
    dest = jnp.sum(jnp.where(sel, offs, 0), axis=-1) + pos
    n_tiles = ends[-1] // TM_X
    tile = jnp.minimum(jnp.arange(N_XTILES, dtype=jnp.int32), n_tiles - 1)
    texp = jnp.sum((ends[None, :] <= (tile * TM_X)[:, None]).astype(jnp.int32), axis=-1)
    return (dest.reshape(T_ALL // TM_R, 1, TOP_K * TM_R).astype(jnp.int32), tile, texp.astype(jnp.int32),
            n_tiles.reshape(1).astype(jnp.int32))


def _row_copy(src, s, dst, d, sem):
    return pltpu.make_async_copy(src.at[pl.ds(s, 1)], dst.at[pl.ds(d, 1)], sem)


def _dispatch_kernel(dest_ref, h_ref, xs_in_ref, xs_ref, sem):
    del xs_in_ref

    def issue(t, carry):
        for k in range(TOP_K):
            _row_copy(h_ref, t, xs_ref, dest_ref[0, TOP_K * t + k], sem).start(priority=k)
        return carry

    lax.fori_loop(0, TM_R, issue, 0, unroll=ROW_DMA_UNROLL)

    def drain(t, carry):
        for k in range(TOP_K):
            _row_copy(h_ref, 0, xs_ref, 0, sem).wait()
        return carry

    lax.fori_loop(0, TM_R, drain, 0, unroll=ROW_DMA_UNROLL)


def _dispatch(dest, h2):
    xs0 = jnp.zeros((N_SLOTS, D), F32)
    return pl.pallas_call(
        _dispatch_kernel,
        grid=(T_ALL // TM_R,),
        in_specs=[
            pl.BlockSpec((None, 1, TOP_K * TM_R), lambda i: (i, 0, 0), memory_space=pltpu.SMEM),
            pl.BlockSpec((TM_R, D), lambda i: (i, 0)),
            pl.BlockSpec(memory_space=pl.ANY),
        ],
        out_specs=pl.BlockSpec(memory_space=pl.ANY),
        out_shape=jax.ShapeDtypeStruct((N_SLOTS, D), F32),
        scratch_shapes=[pltpu.SemaphoreType.DMA],
        input_output_aliases={2: 0},
        compiler_params=_cparams(1),
        name="moe_dispatch",
    )(dest, h2, xs0)


def _experts_kernel(tile_ref, texp_ref, nt_ref, xs_ref, w1_ref, w3_ref, w2_ref, ys_ref):
    del tile_ref, texp_ref
    live = pl.program_id(0) < nt_ref[0]

    @pl.when(live)
    def _():
        ys_ref[...] = _swiglu(xs_ref[...].astype(BF16), w1_ref[...], w3_ref[...], w2_ref[...])

    @pl.when(jnp.logical_not(live))
    def _():
        ys_ref[...] = jnp.zeros_like(ys_ref)


def _experts(tile, texp, n_tiles, xs, w1, w3, w2):
    wspec = lambda shape: pl.BlockSpec((None,) + shape, lambda j, tile, texp, nt: (texp[j], 0, 0))
    rows_in = pl.BlockSpec((TM_X, D), lambda j, tile, texp, nt: (tile[j], 0))
    rows_out = pl.BlockSpec((TM_X, D), lambda j, tile, texp, nt: (j, 0))
    return pl.pallas_call(
        _experts_kernel,
        grid_spec=pltpu.PrefetchScalarGridSpec(
            num_scalar_prefetch=3,
            grid=(N_XTILES,),
            in_specs=[rows_in, wspec((D, D_FF_EXPERT)), wspec((D, D_FF_EXPERT)), wspec((D_FF_EXPERT, D))],
            out_specs=rows_out,
        ),
        out_shape=jax.ShapeDtypeStruct((N_SLOTS, D), F32),
        compiler_params=_cparams(1),
        name="moe_experts",
    )(tile, texp, n_tiles, xs, w1, w3, w2)


def _combine_kernel(dest_ref, ys_ref, route_ref, x_ref, g2_ref, o_ref, y_ref, sem):
    def issue(t, carry):
        for k in range(TOP_K):
            _row_copy(ys_ref, dest_ref[0, TOP_K * t + k], y_ref.at[k], t, sem).start(priority=k)
        return carry

    lax.fori_loop(0, TM_R, issue, 0, unroll=ROW_DMA_UNROLL)

    def drain(t, carry):
        for k in range(TOP_K):
            _row_copy(ys_ref, 0, y_ref.at[k], 0, sem).wait()
        return carry

    lax.fori_loop(0, TM_R, drain, 0, unroll=ROW_DMA_UNROLL)
    route = route_ref[...]
    moe = route[:, R_W0:R_W0 + 1] * y_ref[0] + route[:, R_W1:R_W1 + 1] * y_ref[1]
    o_ref[...] = x_ref[...] + g2_ref[...] * moe


def _combine(dest, ys, route, x1, mod4, layer):
    row = lambda w: pl.BlockSpec((TM_R, w), lambda i: (i, 0))
    return pl.pallas_call(
        _combine_kernel,
        grid=(T_ALL // TM_R,),
        in_specs=[
            pl.BlockSpec((None, 1, TOP_K * TM_R), lambda i: (i, 0, 0), memory_space=pltpu.SMEM),
            pl.BlockSpec(memory_space=pl.ANY),
            row(LANES),
            row(D),
            _mod_spec(layer, 5, TM_R),
        ],
        out_specs=row(D),
        out_shape=jax.ShapeDtypeStruct((T_ALL, D), F32),
        scratch_shapes=[pltpu.VMEM((TOP_K, TM_R, D), F32), pltpu.SemaphoreType.DMA],
        compiler_params=_cparams(1),
        name="moe_combine",
    )(dest, ys, route, x1, mod4)


def _moe(h2, x1, route, counts, mod4, w1, w3, w2, layer):
    dest, tile, texp, n_tiles = _route_plan(route, counts)
    xs = _dispatch(dest, h2)
    ys = _experts(tile, texp, n_tiles, xs, w1, w3, w2)
    return _combine(dest, ys, route, x1, mod4, layer)


TM_NORM = 1024


def _final_norm_kernel(x_ref, w_ref, o_ref):
    x = x_ref[...]
    var = jnp.mean(x * x, axis=-1, keepdims=True)
    o_ref[...] = x * lax.rsqrt(var + EPS) * w_ref[...]


def _final_norm(x, w):
    return pl.pallas_call(
        _final_norm_kernel,
        grid=(T_ALL // TM_NORM,),
        in_specs=[pl.BlockSpec((TM_NORM, D), lambda i: (i, 0)), pl.BlockSpec((1, D), lambda i: (0, 0))],
        out_specs=pl.BlockSpec((TM_NORM, D), lambda i: (i, 0)),
        out_shape=jax.ShapeDtypeStruct((T_ALL, D), F32),
        compiler_params=_cparams(1),
        name="final_norm",
    )(x, w.reshape(1, D))


def _pad_lanes(a, fill=0.0):
    pad = [(0, 0)] * (a.ndim - 1) + [(0, LANES - a.shape[-1])]
    return jnp.pad(a, pad, constant_values=fill)


def _reorder_w_in(w_in):
    a_end = W_A
    s_end = a_end + W_S
    dt_end = s_end + 2 * SSM_HEADS
    w_dt = jnp.pad(w_in[:, :, s_end:dt_end], ((0, 0), (0, 0), (0, W_DT - 2 * SSM_HEADS)))
    return jnp.concatenate([w_in[:, :, :s_end], w_in[:, :, dt_end:], w_dt], axis=-1).astype(BF16)


def kernel(x_prompt, x_sample, cache_k, cache_v, state_ssm, c, c_ctx, norm1_w, norm2_w, w_ada, b_ada, w_in, w_out, conv_a_w, ssm_conv_w, ssm_conv_b, dt_bias, a_log, d_skip, ssm_norm_w, rpb, ffn_w1, ffn_w3, ffn_w2, router_w, router_b, moe_w1, moe_w3, moe_w2, final_norm_w):
    x = jnp.concatenate([x_prompt.reshape(T_CTX, D), x_sample.reshape(T_LAT, D)], axis=0)

    cvec = jnp.concatenate([c_ctx[None, :], c, jnp.zeros((8 - 1 - DEC_BATCH, D), F32)], axis=0)
    mod4 = _adaln_all(cvec, w_ada, b_ada).reshape(DEPTH, 8, 1, N_MOD * D)

    w_in_r = _reorder_w_in(w_in)
    w_out_b = w_out.astype(BF16)
    norm1_w3 = norm1_w.reshape(DEPTH, 1, D)
    norm2_w3 = norm2_w.reshape(DEPTH, 1, D)
    n_dense = ffn_w1.shape[0]
    split_cols = lambda w: jnp.transpose(w.reshape(n_dense, D, 2, D_FF_EXPERT), (0, 2, 1, 3)).astype(BF16)
    ffn_w1c, ffn_w3c = split_cols(ffn_w1), split_cols(ffn_w3)
    ffn_w2c = ffn_w2.reshape(n_dense, 2, D_FF_EXPERT, D).astype(BF16)
    moe_w1b, moe_w3b, moe_w2b = moe_w1.astype(BF16), moe_w3.astype(BF16), moe_w2.astype(BF16)
    router_wp = _pad_lanes(router_w)
    router_bp = _pad_lanes(router_b, NEG)
    dtb = _pad_lanes(dt_bias.reshape(DEPTH, 1, 2 * SSM_HEADS))
    alog = _pad_lanes(a_log.reshape(DEPTH, 1, 2 * SSM_HEADS))
    dsk = jnp.repeat(d_skip, SSM_HEAD_DIM, axis=-1).reshape(DEPTH, 1, D_SSM)
    rpb_flat = jnp.pad(rpb, ((0, 0), (0, 0), (0, 1), (0, GRID_W - rpb.shape[-1])))
    rpb_flat = rpb_flat.reshape(DEPTH, ATT_HEADS, 1, NA_BIAS_FLAT)
    cache_k4 = cache_k.reshape(DEC_BATCH, DEPTH, PAST_LEN, D_ATT)
    cache_v4 = cache_v.reshape(DEC_BATCH, DEPTH, PAST_LEN, D_ATT)
    state5 = state_ssm.reshape(DEC_BATCH, DEPTH, 2, D_SSM, SSM_STATE)

    ks, vs, sts = [], [], []
    for layer in range(DEPTH):
        pa, ps, pqkv, pdt = _inproj(x, mod4, norm1_w3, w_in_r, layer)
        params = (conv_a_w[layer], ssm_conv_w[layer], ssm_conv_b[layer].reshape(1, -1), dtb[layer], alog[layer],
                  dsk[layer], ssm_norm_w[layer].reshape(1, D_SSM))
        yab, st = _mixer_ab(pa, ps, pdt, None, params, SEQ, BATCH, 0, layer)
        yab = _mixer_ab(pa, ps, pdt, state5, params, DEC_SEQ, DEC_BATCH, T_CTX // DEC_SEQ, layer, yab)
        yab = yab.reshape(T_ALL, D_CONV + D_SSM)
        yc = _attn_na(pqkv, cache_k4, cache_v4, rpb_flat, _attn_ctx(pqkv), layer).reshape(T_ALL, D_ATT)
        ks.append(pqkv[:T_CTX, D_ATT:2 * D_ATT].astype(F32).reshape(BATCH, SEQ, ATT_HEADS, ATT_HEAD_DIM))
        vs.append(pqkv[:T_CTX, 2 * D_ATT:].astype(F32).reshape(BATCH, SEQ, ATT_HEADS, ATT_HEAD_DIM))
        sts.append(st.reshape(BATCH, 2, SSM_HEADS, SSM_HEAD_DIM, SSM_STATE))
        i = layer // 2
        if layer % 2 == 0:
            x1, h2 = _outproj(yab, yc, x, mod4, w_out_b, norm2_w3, None, layer)
            x = _ffn(h2, x1, mod4, ffn_w1c[i], ffn_w3c[i], ffn_w2c[i], layer)
        else:
            x1, h2, route, counts = _outproj(yab, yc, x, mod4, w_out_b, norm2_w3,
                                             (router_wp[i], router_bp[i][None, :]), layer)
            x = _moe(h2, x1, route, counts, mod4, moe_w1b[i], moe_w3b[i], moe_w2b[i], layer)

    y = _final_norm(x, final_norm_w)
    y_prompt = y[:T_CTX].reshape(BATCH, SEQ, D)
    y_sample = y[T_CTX:].reshape(DEC_BATCH, DEC_SEQ, D)
    return (y_prompt, y_sample, jnp.stack(ks, axis=1), jnp.stack(vs, axis=1), jnp.stack(sts, axis=1))
```

```python
import functools

import jax
import jax.numpy as jnp
from jax import lax
from jax.experimental import pallas as pl
from jax.experimental.pallas import tpu as pltpu

F32 = jnp.float32
BF16 = jnp.bfloat16

D = 1024
BATCH = 16
SEQ = 256
DEPTH = 4
DEC_BATCH = 4
DEC_SEQ = 1024
PAST_LEN = 512
GRID_W = 64
D_CONV = 256
D_SSM = 384
SSM_HEADS = 6
SSM_HEAD_DIM = 64
SSM_STATE = 128
CHUNK = 128
N_BC = 512
D_ATT = 384
ATT_HEADS = 6
ATT_HEAD_DIM = 64
NA_ROWS = 8
NA_COLS = 16
D_FF = 2816
N_EXPERTS = 8
D_FF_EXPERT = 1408
N_MOD = 6
EPS = 1e-6

T_CTX = BATCH * SEQ
T_LAT = DEC_BATCH * DEC_SEQ
T_ALL = T_CTX + T_LAT

W_A = 3 * D_CONV
W_S = 2 * D_SSM + N_BC
W_QKV = 3 * D_ATT
W_DT = 128
N_PROJ_PAD = W_A + W_S + W_QKV + W_DT

LANES = 128
NEG = -1e30

NA_TQ = 128
NA_TILES = DEC_SEQ // NA_TQ
NA_WIN_ROWS = 10
NA_WIN = NA_WIN_ROWS * GRID_W

VMEM_LIMIT = 56 * 1024 * 1024


def _cparams(n_axes):
    return pltpu.CompilerParams(dimension_semantics=("arbitrary",) * n_axes, vmem_limit_bytes=VMEM_LIMIT)


def _silu(x):
    return x * jax.nn.sigmoid(x)


def _mod_row(i, tm):
    n_ctx = T_CTX // tm
    per_lat = DEC_SEQ // tm
    return jnp.where(i < n_ctx, 0, (i - n_ctx) // per_lat + 1)


def _mod_spec(layer, k, tm):
    return pl.BlockSpec((None, None, 1, D), lambda i, *_: (layer, _mod_row(i, tm), 0, k))


ADA_TN = 1536


def _adaln_kernel(c_ref, w_ref, b_ref, o_ref):
    s = _silu(c_ref[...]).astype(BF16)
    o_ref[...] = jnp.dot(s, w_ref[...].astype(BF16), preferred_element_type=F32) + b_ref[...]


def _adaln_all(cvec, w_ada, b_ada):
    n = N_MOD * D
    return pl.pallas_call(
        _adaln_kernel,
        grid=(DEPTH, n // ADA_TN),
        in_specs=[
            pl.BlockSpec((8, D), lambda l, j: (0, 0)),
            pl.BlockSpec((None, D, ADA_TN), lambda l, j: (l, 0, j)),
            pl.BlockSpec((None, 1, ADA_TN), lambda l, j: (l, 0, j)),
        ],
        out_specs=pl.BlockSpec((None, 8, ADA_TN), lambda l, j: (l, 0, j)),
        out_shape=jax.ShapeDtypeStruct((DEPTH, 8, n), F32),
        compiler_params=_cparams(2),
        name="adaln",
    )(cvec, w_ada, b_ada.reshape(DEPTH, 1, n))


TM_IN = 512


def _norm_mod(x, nw, sh, sc):
    var = jnp.mean(x * x, axis=-1, keepdims=True)
    h = x * lax.rsqrt(var + EPS) * nw
    return h * (1.0 + sc) + sh


def _inproj_kernel(x_ref, nw_ref, sh_ref, sc_ref, w_ref, pa_ref, ps_ref, pqkv_ref, pdt_ref):
    hb = _norm_mod(x_ref[...], nw_ref[...], sh_ref[...], sc_ref[...]).astype(BF16)
    o = 0
    pa_ref[...] = jnp.dot(hb, w_ref[:, o:o + W_A], preferred_element_type=F32).astype(BF16)
    o += W_A
    ps_ref[...] = jnp.dot(hb, w_ref[:, o:o + W_S], preferred_element_type=F32).astype(BF16)
    o += W_S
    pqkv_ref[...] = jnp.dot(hb, w_ref[:, o:o + W_QKV], preferred_element_type=F32).astype(BF16)
    o += W_QKV
    pdt_ref[...] = jnp.dot(hb, w_ref[:, o:o + W_DT], preferred_element_type=F32)


def _inproj(x, mod4, norm1_w3, w_in_r, layer):
    tm = TM_IN
    row = lambda w: pl.BlockSpec((tm, w), lambda i: (i, 0))
    return pl.pallas_call(
        _inproj_kernel,
        grid=(T_ALL // tm,),
        in_specs=[
            row(D),
            pl.BlockSpec((None, 1, D), lambda i: (layer, 0, 0)),
            _mod_spec(layer, 0, tm),
            _mod_spec(layer, 1, tm),
            pl.BlockSpec((None, D, N_PROJ_PAD), lambda i: (layer, 0, 0)),
        ],
        out_specs=[row(W_A), row(W_S), row(W_QKV), row(W_DT)],
        out_shape=[
            jax.ShapeDtypeStruct((T_ALL, W_A), BF16),
            jax.ShapeDtypeStruct((T_ALL, W_S), BF16),
            jax.ShapeDtypeStruct((T_ALL, W_QKV), BF16),
            jax.ShapeDtypeStruct((T_ALL, W_DT), F32),
        ],
        compiler_params=_cparams(1),
        name="inproj",
    )(x, norm1_w3, mod4, mod4, w_in_r)


HALO = 16


def _conv3(cur, prev_row, next_row, w):
    n = cur.shape[0]
    rid = lax.broadcasted_iota(jnp.int32, cur.shape, 0)
    dn = jnp.where(rid == 0, prev_row, pltpu.roll(cur, 1, 0))
    up = jnp.where(rid == n - 1, next_row, pltpu.roll(cur, n - 1, 0))
    return w[0:1] * dn + w[1:2] * cur + w[2:3] * up


def _cumsum_rows(x):
    rid = lax.broadcasted_iota(jnp.int32, x.shape, 0)
    s = 1
    while s < x.shape[0]:
        x = x + jnp.where(rid >= s, pltpu.roll(x, s, 0), 0.0)
        s *= 2
    return x


def _spread_exact(x, onehot):
    hi = x.astype(BF16)
    rest = x - hi.astype(F32)
    mid = rest.astype(BF16)
    lo = (rest - mid.astype(F32)).astype(BF16)
    return (jnp.dot(hi, onehot, preferred_element_type=F32) + jnp.dot(mid, onehot, preferred_element_type=F32)
            + jnp.dot(lo, onehot, preferred_element_type=F32))


def _mixer_ab_kernel(seq_len, has_init, *refs):
    if has_init:
        (pa_ref, ps_ref, pdt_ref, init_ref, wa_ref, cw_ref, cb_ref, dtb_ref, alog_ref, dsk_ref, nw_ref, ctx_ref,
         both_ref, xs_ref, bc_ref, dt_ref, y_ref, rt_ref) = refs
        st_ref = None
        both_ref[0] = ctx_ref[...]
        yab_ref = both_ref.at[1]
    else:
        (pa_ref, ps_ref, pdt_ref, wa_ref, cw_ref, cb_ref, dtb_ref, alog_ref, dsk_ref, nw_ref,
         yab_ref, st_ref, xs_ref, bc_ref, dt_ref, y_ref, rt_ref) = refs
        init_ref = None
    nc = seq_len // CHUNK

    def halo(ref, r0, c):
        lo = jnp.maximum(r0 - HALO, 0)
        hi = jnp.minimum(r0 + CHUNK, seq_len - HALO)
        prev = ref[pl.ds(pl.multiple_of(lo, HALO), HALO), :].astype(F32)[HALO - 1:HALO]
        nxt = ref[pl.ds(pl.multiple_of(hi, HALO), HALO), :].astype(F32)[0:1]
        return jnp.where(c > 0, prev, 0.0), jnp.where(c < nc - 1, nxt, 0.0)

    def prep(c, carry):
        r0 = pl.multiple_of(c * CHUNK, CHUNK)
        rows = pl.ds(r0, CHUNK)
        pa = pa_ref[rows, :].astype(F32)
        pp, pn = halo(pa_ref, r0, c)
        g = pa[:, D_CONV:2 * D_CONV] * pa[:, 2 * D_CONV:]
        gp = pp[:, D_CONV:2 * D_CONV] * pp[:, 2 * D_CONV:]
        gn = pn[:, D_CONV:2 * D_CONV] * pn[:, 2 * D_CONV:]
        ya = pa[:, :D_CONV] * _conv3(g, gp, gn, wa_ref[...])
        yab_ref[rows, 0:D_CONV] = ya.astype(BF16)

        ps = ps_ref[rows, :].astype(F32)
        sp, sn = halo(ps_ref, r0, c)
        cw = cw_ref[...]
        cb = cb_ref[...]
        xs = _conv3(ps[:, :D_SSM], sp[:, :D_SSM], sn[:, :D_SSM], cw[:, :D_SSM]) + cb[:, :D_SSM]
        xs_ref[rows, :] = _silu(xs)
        o = 2 * D_SSM
        bc = _conv3(ps[:, o:], sp[:, o:], sn[:, o:], cw[:, D_SSM:]) + cb[:, D_SSM:]
        bc_ref[rows, :] = _silu(bc)
        t = pdt_ref[rows, :] + dtb_ref[...]
        dt_ref[rows, :] = jnp.maximum(t, 0.0) + jnp.log1p(jnp.exp(-jnp.abs(t)))
        return carry

    lax.fori_loop(0, nc, prep, 0)

    for d in range(2):
        for j in range(D_SSM // LANES):
            cols = slice(j * LANES, (j + 1) * LANES)
            if has_init:
                rt_ref[d, :, cols] = init_ref[d, cols, :].T
            else:
                rt_ref[d, :, cols] = jnp.zeros((SSM_STATE, LANES), F32)

    a_row = -jnp.exp(alog_ref[...])

    def ssd_chunk(c, d):
        rows = pl.ds(pl.multiple_of(c * CHUNK, CHUNK), CHUNK)
        xs = xs_ref[rows, :]
        bc = bc_ref[rows, :]
        dt = dt_ref[rows, :]
        adt = dt * a_row
        cs = _cumsum_rows(adt)
        total = cs[CHUNK - 1:CHUNK, :]
        e = cs if d == 0 else total - cs + adt
        e_t = e.T
        dec = jnp.exp(total - e)
        ee = jnp.exp(e)
        lane_k = lax.broadcasted_iota(jnp.int32, (LANES, D_SSM), 0)
        to_cols = jnp.where(lane_k == d * SSM_HEADS + lax.broadcasted_iota(jnp.int32, (LANES, D_SSM), 1)
                            // SSM_HEAD_DIM, 1.0, 0.0).astype(BF16)
        lane_k2 = lax.broadcasted_iota(jnp.int32, (LANES, SSM_HEADS * CHUNK), 0)
        to_blocks = jnp.where(lane_k2 == d * SSM_HEADS
                              + lax.broadcasted_iota(jnp.int32, (LANES, SSM_HEADS * CHUNK), 1) // CHUNK,
                              1.0, 0.0).astype(BF16)
        spread = _spread_exact(jnp.concatenate([dt, ee, dec], axis=0), to_cols)
        dt_x, ee_x, dec_x = spread[:CHUNK], spread[CHUNK:2 * CHUNK], spread[2 * CHUNK:]
        etot_x = ee_x[CHUNK - 1:CHUNK] if d == 0 else ee_x[0:1]
        e_blk = _spread_exact(e, to_blocks)
        xdt = xs * dt_x
        xdt_b = xdt.astype(BF16)
        xdd_b = (xdt * dec_x).astype(BF16)
        rt = rt_ref[d]
        rt_b = rt.astype(BF16)
        ri = lax.broadcasted_iota(jnp.int32, (CHUNK, CHUNK), 0)
        ci = lax.broadcasted_iota(jnp.int32, (CHUNK, CHUNK), 1)
        valid = (ri >= ci) if d == 0 else (ri <= ci)
        col_head = lax.broadcasted_iota(jnp.int32, (CHUNK, D_SSM), 1) // SSM_HEAD_DIM
        heads_per_group = SSM_HEADS // 2
        heads_per_tile = LANES // SSM_HEAD_DIM
        tile_head = lax.broadcasted_iota(jnp.int32, (CHUNK, LANES), 1) // SSM_HEAD_DIM
        y_tiles = [jnp.zeros((CHUNK, LANES), F32) for _ in range(D_SSM // LANES)]
        y_off = jnp.zeros((CHUNK, D_SSM), F32)
        st = jnp.zeros((SSM_STATE, D_SSM), F32)
        for g in range(2):
            in_group = col_head // heads_per_group == g
            bg = bc[:, g * SSM_STATE:(g + 1) * SSM_STATE]
            cg = bc[:, 2 * SSM_STATE + g * SSM_STATE:2 * SSM_STATE + (g + 1) * SSM_STATE].astype(BF16)
            cb = lax.dot_general(cg, bg.astype(BF16), (((1,), (1,)), ((), ())), preferred_element_type=F32)
            bg_t = bg.T.astype(BF16)
            y_off = y_off + jnp.dot(cg, jnp.where(in_group, rt_b, jnp.zeros_like(rt_b)),
                                    preferred_element_type=F32)
            st = jnp.where(in_group, jnp.dot(bg_t, xdd_b, preferred_element_type=F32), st)
            for hh in range(heads_per_group):
                h = g * heads_per_group + hh
                k = d * SSM_HEADS + h
                seg = e_blk[:, h * CHUNK:(h + 1) * CHUNK] - e_t[k:k + 1, :]
                lm = jnp.where(valid, jnp.exp(jnp.where(valid, seg, 0.0)), 0.0)
                j = h // heads_per_tile
                x_tile = xdt_b[:, j * LANES:(j + 1) * LANES]
                x_tile = jnp.where(tile_head == h % heads_per_tile, x_tile, jnp.zeros_like(x_tile))
                y_tiles[j] = y_tiles[j] + jnp.dot((cb * lm).astype(BF16), x_tile, preferred_element_type=F32)
        y = jnp.concatenate(y_tiles, axis=1) + y_off * ee_x
        if d == 0:
            y_ref[rows, :] = y
        else:
            y_ref[rows, :] = y_ref[rows, :] + y
        rt_ref[d] = rt * etot_x + st

    lax.fori_loop(0, nc, lambda i, cr: (ssd_chunk(i, 0), cr)[1], 0)
    lax.fori_loop(0, nc, lambda i, cr: (ssd_chunk(nc - 1 - i, 1), cr)[1], 0)

    if not has_init:
        for d in range(2):
            for j in range(D_SSM // LANES):
                cols = slice(j * LANES, (j + 1) * LANES)
                st_ref[d, cols, :] = rt_ref[d, :, cols].T

    def finish(c, carry):
        rows = pl.ds(pl.multiple_of(c * CHUNK, CHUNK), CHUNK)
        y = y_ref[rows, :] + dsk_ref[...] * xs_ref[rows, :]
        y = y * _silu(ps_ref[rows, D_SSM:2 * D_SSM].astype(F32))
        col = lax.broadcasted_iota(jnp.int32, y.shape, 1)
        first = col < D_SSM // 2
        ysq = y * y
        s0 = jnp.sum(jnp.where(first, ysq, 0.0), axis=-1, keepdims=True)
        s1 = jnp.sum(jnp.where(first, 0.0, ysq), axis=-1, keepdims=True)
        inv = 1.0 / (D_SSM // 2)
        r = jnp.where(first, lax.rsqrt(s0 * inv + EPS), lax.rsqrt(s1 * inv + EPS))
        yab_ref[rows, D_CONV:] = (y * r * nw_ref[...]).astype(BF16)
        return carry

    lax.fori_loop(0, nc, finish, 0)


def _mixer_ab(pa, ps, pdt, init, params, seq_len, n_seq, blk0, layer, yab=None):
    has_init = init is not None
    n_w = D_CONV + D_SSM
    seq = lambda w: pl.BlockSpec((seq_len, w), lambda s: (blk0 + s, 0))
    full = lambda a: pl.BlockSpec(a.shape, lambda s: (0,) * a.ndim)
    in_specs = [seq(W_A), seq(W_S), seq(W_DT)]
    args = [pa, ps, pdt]
    if has_init:
        in_specs.append(pl.BlockSpec((None, None, 2, D_SSM, SSM_STATE), lambda s: (s, layer, 0, 0, 0)))
        args.append(init)
    in_specs += [full(p) for p in params]
    args += list(params)
    if has_init:
        assert T_CTX == T_LAT
        in_specs.append(pl.BlockSpec((seq_len, n_w), lambda s: (s, 0)))
        args.append(yab)
        out_specs = pl.BlockSpec((2, seq_len, n_w), lambda s: (0, s, 0))
        out_shape = jax.ShapeDtypeStruct((2, T_CTX, n_w), BF16)
    else:
        out_specs = [pl.BlockSpec((seq_len, n_w), lambda s: (s, 0)),
                     pl.BlockSpec((None, 2, D_SSM, SSM_STATE), lambda s: (s, 0, 0, 0))]
        out_shape = [jax.ShapeDtypeStruct((T_CTX, n_w), BF16),
                     jax.ShapeDtypeStruct((n_seq, 2, D_SSM, SSM_STATE), F32)]
    return pl.pallas_call(
        functools.partial(_mixer_ab_kernel, seq_len, has_init),
        grid=(n_seq,),
        in_specs=in_specs,
        out_specs=out_specs,
        out_shape=out_shape,
        scratch_shapes=[
            pltpu.VMEM((seq_len, D_SSM), F32),
            pltpu.VMEM((seq_len, N_BC), F32),
            pltpu.VMEM((seq_len, LANES), F32),
            pltpu.VMEM((seq_len, D_SSM), F32),
            pltpu.VMEM((2, SSM_STATE, D_SSM), F32),
        ],
        compiler_params=_cparams(1),
        name="mixer_ab_lat" if has_init else "mixer_ab_ctx",
    )(*args)


ATT_SCALE = ATT_HEAD_DIM ** -0.5


def _dot_nt(a, b):
    return lax.dot_general(a, b, (((1,), (1,)), ((), ())), preferred_element_type=F32)


def _attn_ctx_kernel(qkv_ref, o_ref):
    for h in range(ATT_HEADS):
        hc = slice(h * ATT_HEAD_DIM, (h + 1) * ATT_HEAD_DIM)
        q = qkv_ref[:, h * ATT_HEAD_DIM:(h + 1) * ATT_HEAD_DIM]
        k = qkv_ref[:, D_ATT + h * ATT_HEAD_DIM:D_ATT + (h + 1) * ATT_HEAD_DIM]
        v = qkv_ref[:, 2 * D_ATT + h * ATT_HEAD_DIM:2 * D_ATT + (h + 1) * ATT_HEAD_DIM]
        s = _dot_nt(q, k) * ATT_SCALE
        p = jnp.exp(s - jnp.max(s, axis=-1, keepdims=True))
        l = jnp.sum(p, axis=-1, keepdims=True)
        o = jnp.dot(p.astype(BF16), v, preferred_element_type=F32)
        o_ref[:, hc] = (o / l).astype(BF16)


def _attn_ctx(pqkv):
    return pl.pallas_call(
        _attn_ctx_kernel,
        grid=(BATCH,),
        in_specs=[pl.BlockSpec((SEQ, W_QKV), lambda s: (s, 0))],
        out_specs=pl.BlockSpec((SEQ, D_ATT), lambda s: (s, 0)),
        out_shape=jax.ShapeDtypeStruct((T_CTX, D_ATT), BF16),
        compiler_params=_cparams(1),
        name="attn_ctx",
    )(pqkv)


def _na_win_start(i):
    return jnp.clip(2 * i - NA_ROWS // 2, 0, DEC_SEQ // GRID_W - NA_WIN_ROWS)


NA_BIAS_FLAT = 1024


def _attn_na_kernel(qkv_ref, kc_ref, vc_ref, rpb_ref, ctx_ref, both_ref, bias_ref):
    both_ref[0] = ctx_ref[...]
    o_ref = both_ref.at[1]
    i = pl.program_id(1)
    q0 = pl.multiple_of(i * NA_TQ, NA_TQ)
    w_row = _na_win_start(i)
    w0 = pl.multiple_of(w_row * GRID_W, 2 * GRID_W)

    @pl.when(pl.program_id(0) == 0)
    def _():
        ql = lax.broadcasted_iota(jnp.int32, (NA_TQ, NA_WIN), 0) + q0
        kl = lax.broadcasted_iota(jnp.int32, (NA_TQ, NA_WIN), 1) + w0
        qr, qc = ql // GRID_W, ql % GRID_W
        kr, kc = kl // GRID_W, kl % GRID_W
        rs = jnp.clip(qr - NA_ROWS // 2, 0, DEC_SEQ // GRID_W - NA_ROWS)
        cs = jnp.clip(qc - NA_COLS // 2, 0, GRID_W - NA_COLS)
        valid = (kr >= rs) & (kr < rs + NA_ROWS) & (kc >= cs) & (kc < cs + NA_COLS)
        centre = (NA_ROWS - 1) * GRID_W + NA_COLS - 1
        shift = lax.rem(NA_BIAS_FLAT - (w0 - q0 + centre), NA_BIAS_FLAT)
        for h in range(ATT_HEADS):
            table = jnp.broadcast_to(rpb_ref[h], (NA_TQ, NA_BIAS_FLAT))
            rolled = pltpu.roll(table, shift, 1, stride=1, stride_axis=0)
            bias_ref[i, h] = jnp.where(valid, rolled[:, :NA_WIN], NEG)

    half = lambda n: lax.broadcasted_iota(jnp.int32, (n, LANES), 1) // ATT_HEAD_DIM
    half_q, half_loc, half_ctx = half(NA_TQ), half(NA_WIN), half(PAST_LEN)
    for j in range(ATT_HEADS * ATT_HEAD_DIM // LANES):
        t0 = j * LANES
        q_t = qkv_ref[pl.ds(q0, NA_TQ), t0:t0 + LANES]
        k_loc = qkv_ref[pl.ds(w0, NA_WIN), D_ATT + t0:D_ATT + t0 + LANES]
        v_loc = qkv_ref[pl.ds(w0, NA_WIN), 2 * D_ATT + t0:2 * D_ATT + t0 + LANES]
        k_ctx = kc_ref[:, t0:t0 + LANES].astype(BF16)
        v_ctx = vc_ref[:, t0:t0 + LANES].astype(BF16)
        out_t = jnp.zeros((NA_TQ, LANES), F32)
        for hh in range(LANES // ATT_HEAD_DIM):
            h = j * (LANES // ATT_HEAD_DIM) + hh
            q = jnp.where(half_q == hh, q_t, jnp.zeros_like(q_t))
            s_loc = _dot_nt(q, k_loc) * ATT_SCALE + bias_ref[i, h]
            s_ctx = _dot_nt(q, k_ctx) * ATT_SCALE
            m = jnp.maximum(jnp.max(s_loc, axis=-1, keepdims=True), jnp.max(s_ctx, axis=-1, keepdims=True))
            p_loc = jnp.exp(s_loc - m).astype(BF16)
            p_ctx = jnp.exp(s_ctx - m).astype(BF16)
            o = (jnp.dot(p_loc, jnp.where(half_loc == hh, v_loc, jnp.ones_like(v_loc)), preferred_element_type=F32)
                 + jnp.dot(p_ctx, jnp.where(half_ctx == hh, v_ctx, jnp.ones_like(v_ctx)),
                           preferred_element_type=F32))
            denom = pltpu.roll(o, ATT_HEAD_DIM, 1)
            out_t = jnp.where(half_q == hh, o / denom, out_t)
        o_ref[:, t0:t0 + LANES] = out_t.astype(BF16)


def _attn_na(pqkv, cache_k4, cache_v4, rpb_flat, yc_ctx, layer):
    assert T_CTX == T_LAT
    lat0 = T_CTX // DEC_SEQ
    return pl.pallas_call(
        _attn_na_kernel,
        grid=(DEC_BATCH, NA_TILES),
        in_specs=[
            pl.BlockSpec((DEC_SEQ, W_QKV), lambda b, i: (lat0 + b, 0)),
            pl.BlockSpec((None, None, PAST_LEN, D_ATT), lambda b, i: (b, layer, 0, 0)),
            pl.BlockSpec((None, None, PAST_LEN, D_ATT), lambda b, i: (b, layer, 0, 0)),
            pl.BlockSpec((None, ATT_HEADS, 1, NA_BIAS_FLAT), lambda b, i: (layer, 0, 0, 0)),
            pl.BlockSpec((NA_TQ, D_ATT), lambda b, i: (b * NA_TILES + i, 0)),
        ],
        out_specs=pl.BlockSpec((2, NA_TQ, D_ATT), lambda b, i: (0, b * NA_TILES + i, 0)),
        out_shape=jax.ShapeDtypeStruct((2, T_CTX, D_ATT), BF16),
        scratch_shapes=[pltpu.VMEM((NA_TILES, ATT_HEADS, NA_TQ, NA_WIN), F32)],
        compiler_params=_cparams(2),
        name="attn_na",
    )(pqkv, cache_k4, cache_v4, rpb_flat, yc_ctx)


TM_OUT = 512


R_E0, R_E1, R_W0, R_W1, R_P0, R_P1 = range(6)


def _outproj_kernel(with_router, *refs):
    if with_router:
        (yab_ref, yc_ref, x_ref, w_ref, g1_ref, nw_ref, sh_ref, sc_ref, rw_ref, rb_ref,
         x1_ref, h2_ref, route_ref, count_ref, base_ref) = refs
    else:
        yab_ref, yc_ref, x_ref, w_ref, g1_ref, nw_ref, sh_ref, sc_ref, x1_ref, h2_ref = refs
    n_ab = D_CONV + D_SSM
    out = (jnp.dot(yab_ref[...], w_ref[0:n_ab, :], preferred_element_type=F32)
           + jnp.dot(yc_ref[...], w_ref[n_ab:, :], preferred_element_type=F32))
    x1 = x_ref[...] + g1_ref[...] * out
    x1_ref[...] = x1
    h2 = _norm_mod(x1, nw_ref[...], sh_ref[...], sc_ref[...])
    h2_ref[...] = h2.astype(h2_ref.dtype)
    if with_router:
        @pl.when(pl.program_id(0) == 0)
        def _():
            base_ref[...] = jnp.zeros_like(base_ref)

        rw = rw_ref[...]
        h_hi, w_hi = h2.astype(BF16), rw.astype(BF16)
        h_lo, w_lo = (h2 - h_hi.astype(F32)).astype(BF16), (rw - w_hi.astype(F32)).astype(BF16)
        logits = (jnp.dot(h_hi, w_hi, preferred_element_type=F32) + jnp.dot(h_hi, w_lo, preferred_element_type=F32)
                  + jnp.dot(h_lo, w_hi, preferred_element_type=F32))
        logits = logits + rb_ref[...]
        lane = lax.broadcasted_iota(jnp.int32, logits.shape, 1)
        m1 = jnp.max(logits, axis=-1, keepdims=True)
        i1 = jnp.min(jnp.where(logits == m1, lane, LANES), axis=-1, keepdims=True)
        rest = jnp.where(lane == i1, -jnp.inf, logits)
        m2 = jnp.max(rest, axis=-1, keepdims=True)
        i2 = jnp.min(jnp.where(rest == m2, lane, LANES), axis=-1, keepdims=True)
        e2 = jnp.exp(m2 - m1)
        den = 1.0 + e2
        tm = logits.shape[0]
        sel0, sel1 = lane == i1, lane == i2
        picked = jnp.where(sel0 | sel1, 1.0, 0.0)
        earlier = (lax.broadcasted_iota(jnp.int32, (tm, tm), 1)
                   < lax.broadcasted_iota(jnp.int32, (tm, tm), 0)).astype(BF16)
        rank = base_ref[...] + jnp.dot(earlier, picked.astype(BF16), preferred_element_type=F32)
        p0 = jnp.sum(jnp.where(sel0, rank, 0.0), axis=-1, keepdims=True)
        p1 = jnp.sum(jnp.where(sel1, rank, 0.0), axis=-1, keepdims=True)
        base = base_ref[...] + jnp.sum(picked, axis=0, keepdims=True)
        base_ref[...] = base
        count_ref[...] = base
        rec = jnp.zeros(logits.shape, F32)
        for k, val in ((R_E0, i1.astype(F32)), (R_E1, i2.astype(F32)), (R_W0, 1.0 / den), (R_W1, e2 / den),
                       (R_P0, p0), (R_P1, p1)):
            rec = jnp.where(lane == k, val, rec)
        route_ref[...] = rec


def _outproj(yab, yc, x, mod4, w_out_b, norm2_w3, router, layer):
    tm = TM_OUT
    with_router = router is not None
    row = lambda w: pl.BlockSpec((tm, w), lambda i: (i, 0))
    in_specs = [
        row(D_CONV + D_SSM), row(D_ATT), row(D),
        pl.BlockSpec((None, D, D), lambda i: (layer, 0, 0)),
        _mod_spec(layer, 2, tm),
        pl.BlockSpec((None, 1, D), lambda i: (layer, 0, 0)),
        _mod_spec(layer, 3, tm),
        _mod_spec(layer, 4, tm),
    ]
    args = [yab, yc, x, w_out_b, mod4, norm2_w3, mod4, mod4]
    out_specs = [row(D), row(D)]
    out_shape = [jax.ShapeDtypeStruct((T_ALL, D), F32), jax.ShapeDtypeStruct((T_ALL, D), F32 if with_router else BF16)]
    scratch = []
    if with_router:
        rw, rb = router
        in_specs += [pl.BlockSpec(rw.shape, lambda i: (0, 0)), pl.BlockSpec(rb.shape, lambda i: (0, 0))]
        args += [rw, rb]
        out_specs += [row(LANES), pl.BlockSpec((1, LANES), lambda i: (0, 0))]
        out_shape += [jax.ShapeDtypeStruct((T_ALL, LANES), F32), jax.ShapeDtypeStruct((1, LANES), F32)]
        scratch = [pltpu.VMEM((1, LANES), F32)]
    return pl.pallas_call(
        functools.partial(_outproj_kernel, with_router),
        grid=(T_ALL // tm,),
        in_specs=in_specs,
        out_specs=out_specs,
        out_shape=out_shape,
        scratch_shapes=scratch,
        compiler_params=_cparams(1),
        name="outproj_router" if with_router else "outproj",
    )(*args)


TM_FF = 512


def _swiglu(h, w1, w3, w2):
    u = jnp.dot(h, w1, preferred_element_type=F32)
    v = jnp.dot(h, w3, preferred_element_type=F32)
    return jnp.dot((_silu(u) * v).astype(BF16), w2, preferred_element_type=F32)


def _ffn_kernel(h_ref, w1_ref, w3_ref, w2_ref, x_ref, g2_ref, o_ref, acc_ref):
    e = pl.program_id(1)

    @pl.when(e == 0)
    def _():
        acc_ref[...] = jnp.zeros_like(acc_ref)

    acc_ref[...] += _swiglu(h_ref[...], w1_ref[...], w3_ref[...], w2_ref[...])

    @pl.when(e == pl.num_programs(1) - 1)
    def _():
        o_ref[...] = x_ref[...] + g2_ref[...] * acc_ref[...]


def _ffn(h2, x1, mod4, w1, w3, w2, layer):
    tm = TM_FF
    n_chunks = w1.shape[0]
    row = lambda w: pl.BlockSpec((tm, w), lambda i, e: (i, 0))
    return pl.pallas_call(
        _ffn_kernel,
        grid=(T_ALL // tm, n_chunks),
        in_specs=[
            row(D),
            pl.BlockSpec((None, D, D_FF_EXPERT), lambda i, e: (e, 0, 0)),
            pl.BlockSpec((None, D, D_FF_EXPERT), lambda i, e: (e, 0, 0)),
            pl.BlockSpec((None, D_FF_EXPERT, D), lambda i, e: (e, 0, 0)),
            row(D),
            _mod_spec(layer, 5, tm),
        ],
        out_specs=row(D),
        out_shape=jax.ShapeDtypeStruct((T_ALL, D), F32),
        scratch_shapes=[pltpu.VMEM((tm, D), F32)],
        compiler_params=_cparams(2),
        name="ffn",
    )(h2, w1, w3, w2, x1, mod4)


TOP_K = 2
TM_X = 512
TM_R = 512
ROW_DMA_UNROLL = 8
N_SLOTS = TOP_K * T_ALL + N_EXPERTS * TM_X
N_XTILES = N_SLOTS // TM_X


def _route_plan(route, counts):
    cnt = counts[0, :N_EXPERTS].astype(jnp.int32)
    gsz = (cnt + TM_X - 1) // TM_X * TM_X
    ends = jnp.cumsum(gsz)
    offs = ends - gsz
    e = route[:, R_E0:R_E1 + 1].astype(jnp.int32)
    pos = route[:, R_P0:R_P1 + 1].astype(jnp.int32)
    sel = e[..., None] == jnp.arange(N_EXPERTS, dtype=jnp.int32)
    dest = jnp.sum(jnp.where(sel, offs, 0), axis=-1) + pos
    n_tiles = ends[-1] // TM_X
    tile = jnp.minimum(jnp.arange(N_XTILES, dtype=jnp.int32), n_tiles - 1)
    texp = jnp.sum((ends[None, :] <= (tile * TM_X)[:, None]).astype(jnp.int32), axis=-1)
    return (dest.reshape(T_ALL // TM_R, 1, TOP_K * TM_R).astype(jnp.int32), tile, texp.astype(jnp.int32),
            n_tiles.reshape(1).astype(jnp.int32))


def _row_copy(src, s, dst, d, sem):
    return pltpu.make_async_copy(src.at[pl.ds(s, 1)], dst.at[pl.ds(d, 1)], sem)


def _dispatch_kernel(dest_ref, h_ref, xs_in_ref, xs_ref, sem):
    del xs_in_ref

    def issue(t, carry):
        for k in range(TOP_K):
            _row_copy(h_ref, t, xs_ref, dest_ref[0, TOP_K * t + k], sem).start(priority=k)
        return carry

    lax.fori_loop(0, TM_R, issue, 0, unroll=ROW_DMA_UNROLL)

    def drain(t, carry):
        for k in range(TOP_K):
            _row_copy(h_ref, 0, xs_ref, 0, sem).wait()
        return carry

    lax.fori_loop(0, TM_R, drain, 0, unroll=ROW_DMA_UNROLL)


def _dispatch(dest, h2):
    xs0 = jnp.zeros((N_SLOTS, D), F32)
    return pl.pallas_call(
        _dispatch_kernel,
        grid=(T_ALL // TM_R,),
        in_specs=[
            pl.BlockSpec((None, 1, TOP_K * TM_R), lambda i: (i, 0, 0), memory_space=pltpu.SMEM),
            pl.BlockSpec((TM_R, D), lambda i: (i, 0)),
            pl.BlockSpec(memory_space=pl.ANY),
        ],
        out_specs=pl.BlockSpec(memory_space=pl.ANY),
        out_shape=jax.ShapeDtypeStruct((N_SLOTS, D), F32),
        scratch_shapes=[pltpu.SemaphoreType.DMA],
        input_output_aliases={2: 0},
        compiler_params=_cparams(1),
        name="moe_dispatch",
    )(dest, h2, xs0)


def _experts_kernel(tile_ref, texp_ref, nt_ref, xs_ref, w1_ref, w3_ref, w2_ref, ys_ref):
    del tile_ref, texp_ref
    live = pl.program_id(0) < nt_ref[0]

    @pl.when(live)
    def _():
        ys_ref[...] = _swiglu(xs_ref[...].astype(BF16), w1_ref[...], w3_ref[...], w2_ref[...])

    @pl.when(jnp.logical_not(live))
    def _():
        ys_ref[...] = jnp.zeros_like(ys_ref)


def _experts(tile, texp, n_tiles, xs, w1, w3, w2):
    wspec = lambda shape: pl.BlockSpec((None,) + shape, lambda j, tile, texp, nt: (texp[j], 0, 0))
    rows_in = pl.BlockSpec((TM_X, D), lambda j, tile, texp, nt: (tile[j], 0))
    rows_out = pl.BlockSpec((TM_X, D), lambda j, tile, texp, nt: (j, 0))
    return pl.pallas_call(
        _experts_kernel,
        grid_spec=pltpu.PrefetchScalarGridSpec(
            num_scalar_prefetch=3,
            grid=(N_XTILES,),
            in_specs=[rows_in, wspec((D, D_FF_EXPERT)), wspec((D, D_FF_EXPERT)), wspec((D_FF_EXPERT, D))],
            out_specs=rows_out,
        ),
        out_shape=jax.ShapeDtypeStruct((N_SLOTS, D), F32),
        compiler_params=_cparams(1),
        name="moe_experts",
    )(tile, texp, n_tiles, xs, w1, w3, w2)


def _combine_kernel(dest_ref, ys_ref, route_ref, x_ref, g2_ref, o_ref, y_ref, sem):
    def issue(t, carry):
        for k in range(TOP_K):
            _row_copy(ys_ref, dest_ref[0, TOP_K * t + k], y_ref.at[k], t, sem).start(priority=k)
        return carry

    lax.fori_loop(0, TM_R, issue, 0, unroll=ROW_DMA_UNROLL)

    def drain(t, carry):
        for k in range(TOP_K):
            _row_copy(ys_ref, 0, y_ref.at[k], 0, sem).wait()
        return carry

    lax.fori_loop(0, TM_R, drain, 0, unroll=ROW_DMA_UNROLL)
    route = route_ref[...]
    moe = route[:, R_W0:R_W0 + 1] * y_ref[0] + route[:, R_W1:R_W1 + 1] * y_ref[1]
    o_ref[...] = x_ref[...] + g2_ref[...] * moe


def _combine(dest, ys, route, x1, mod4, layer):
    row = lambda w: pl.BlockSpec((TM_R, w), lambda i: (i, 0))
    return pl.pallas_call(
        _combine_kernel,
        grid=(T_ALL // TM_R,),
        in_specs=[
            pl.BlockSpec((None, 1, TOP_K * TM_R), lambda i: (i, 0, 0), memory_space=pltpu.SMEM),
            pl.BlockSpec(memory_space=pl.ANY),
            row(LANES),
            row(D),
            _mod_spec(layer, 5, TM_R),
        ],
        out_specs=row(D),
        out_shape=jax.ShapeDtypeStruct((T_ALL, D), F32),
        scratch_shapes=[pltpu.VMEM((TOP_K, TM_R, D), F32), pltpu.SemaphoreType.DMA],
        compiler_params=_cparams(1),
        name="moe_combine",
    )(dest, ys, route, x1, mod4)


def _moe(h2, x1, route, counts, mod4, w1, w3, w2, layer):
    dest, tile, texp, n_tiles = _route_plan(route, counts)
    xs = _dispatch(dest, h2)
    ys = _experts(tile, texp, n_tiles, xs, w1, w3, w2)
    return _combine(dest, ys, route, x1, mod4, layer)


TM_NORM = 1024


def _final_norm_kernel(x_ref, w_ref, o_ref):
    x = x_ref[...]
    var = jnp.mean(x * x, axis=-1, keepdims=True)
    o_ref[...] = x * lax.rsqrt(var + EPS) * w_ref[...]


def _final_norm(x, w):
    return pl.pallas_call(
        _final_norm_kernel,
        grid=(T_ALL // TM_NORM,),
        in_specs=[pl.BlockSpec((TM_NORM, D), lambda i: (i, 0)), pl.BlockSpec((1, D), lambda i: (0, 0))],
        out_specs=pl.BlockSpec((TM_NORM, D), lambda i: (i, 0)),
        out_shape=jax.ShapeDtypeStruct((T_ALL, D), F32),
        compiler_params=_cparams(1),
        name="final_norm",
    )(x, w.reshape(1, D))


def _pad_lanes(a, fill=0.0):
    pad = [(0, 0)] * (a.ndim - 1) + [(0, LANES - a.shape[-1])]
    return jnp.pad(a, pad, constant_values=fill)


def _reorder_w_in(w_in):
    a_end = W_A
    s_end = a_end + W_S
    dt_end = s_end + 2 * SSM_HEADS
    w_dt = jnp.pad(w_in[:, :, s_end:dt_end], ((0, 0), (0, 0), (0, W_DT - 2 * SSM_HEADS)))
    return jnp.concatenate([w_in[:, :, :s_end], w_in[:, :, dt_end:], w_dt], axis=-1).astype(BF16)


def kernel(x_prompt, x_sample, cache_k, cache_v, state_ssm, c, c_ctx, norm1_w, norm2_w, w_ada, b_ada, w_in, w_out, conv_a_w, ssm_conv_w, ssm_conv_b, dt_bias, a_log, d_skip, ssm_norm_w, rpb, ffn_w1, ffn_w3, ffn_w2, router_w, router_b, moe_w1, moe_w3, moe_w2, final_norm_w):
    x = jnp.concatenate([x_prompt.reshape(T_CTX, D), x_sample.reshape(T_LAT, D)], axis=0)

    cvec = jnp.concatenate([c_ctx[None, :], c, jnp.zeros((8 - 1 - DEC_BATCH, D), F32)], axis=0)
    mod4 = _adaln_all(cvec, w_ada, b_ada).reshape(DEPTH, 8, 1, N_MOD * D)

    w_in_r = _reorder_w_in(w_in)
    w_out_b = w_out.astype(BF16)
    norm1_w3 = norm1_w.reshape(DEPTH, 1, D)
    norm2_w3 = norm2_w.reshape(DEPTH, 1, D)
    n_dense = ffn_w1.shape[0]
    split_cols = lambda w: jnp.transpose(w.reshape(n_dense, D, 2, D_FF_EXPERT), (0, 2, 1, 3)).astype(BF16)
    ffn_w1c, ffn_w3c = split_cols(ffn_w1), split_cols(ffn_w3)
    ffn_w2c = ffn_w2.reshape(n_dense, 2, D_FF_EXPERT, D).astype(BF16)
    moe_w1b, moe_w3b, moe_w2b = moe_w1.astype(BF16), moe_w3.astype(BF16), moe_w2.astype(BF16)
    router_wp = _pad_lanes(router_w)
    router_bp = _pad_lanes(router_b, NEG)
    dtb = _pad_lanes(dt_bias.reshape(DEPTH, 1, 2 * SSM_HEADS))
    alog = _pad_lanes(a_log.reshape(DEPTH, 1, 2 * SSM_HEADS))
    dsk = jnp.repeat(d_skip, SSM_HEAD_DIM, axis=-1).reshape(DEPTH, 1, D_SSM)
    rpb_flat = jnp.pad(rpb, ((0, 0), (0, 0), (0, 1), (0, GRID_W - rpb.shape[-1])))
    rpb_flat = rpb_flat.reshape(DEPTH, ATT_HEADS, 1, NA_BIAS_FLAT)
    cache_k4 = cache_k.reshape(DEC_BATCH, DEPTH, PAST_LEN, D_ATT)
    cache_v4 = cache_v.reshape(DEC_BATCH, DEPTH, PAST_LEN, D_ATT)
    state5 = state_ssm.reshape(DEC_BATCH, DEPTH, 2, D_SSM, SSM_STATE)

    ks, vs, sts = [], [], []
    for layer in range(DEPTH):
        pa, ps, pqkv, pdt = _inproj(x, mod4, norm1_w3, w_in_r, layer)
        params = (conv_a_w[layer], ssm_conv_w[layer], ssm_conv_b[layer].reshape(1, -1), dtb[layer], alog[layer],
                  dsk[layer], ssm_norm_w[layer].reshape(1, D_SSM))
        yab, st = _mixer_ab(pa, ps, pdt, None, params, SEQ, BATCH, 0, layer)
        yab = _mixer_ab(pa, ps, pdt, state5, params, DEC_SEQ, DEC_BATCH, T_CTX // DEC_SEQ, layer, yab)
        yab = yab.reshape(T_ALL, D_CONV + D_SSM)
        yc = _attn_na(pqkv, cache_k4, cache_v4, rpb_flat, _attn_ctx(pqkv), layer).reshape(T_ALL, D_ATT)
        ks.append(pqkv[:T_CTX, D_ATT:2 * D_ATT].astype(F32).reshape(BATCH, SEQ, ATT_HEADS, ATT_HEAD_DIM))
        vs.append(pqkv[:T_CTX, 2 * D_ATT:].astype(F32).reshape(BATCH, SEQ, ATT_HEADS, ATT_HEAD_DIM))
        sts.append(st.reshape(BATCH, 2, SSM_HEADS, SSM_HEAD_DIM, SSM_STATE))
        i = layer // 2
        if layer % 2 == 0:
            x1, h2 = _outproj(yab, yc, x, mod4, w_out_b, norm2_w3, None, layer)
            x = _ffn(h2, x1, mod4, ffn_w1c[i], ffn_w3c[i], ffn_w2c[i], layer)
        else:
            x1, h2, route, counts = _outproj(yab, yc, x, mod4, w_out_b, norm2_w3,
                                             (router_wp[i], router_bp[i][None, :]), layer)
            x = _moe(h2, x1, route, counts, mod4, moe_w1b[i], moe_w3b[i], moe_w2b[i], layer)

    y = _final_norm(x, final_norm_w)
    y_prompt = y[:T_CTX].reshape(BATCH, SEQ, D)
    y_sample = y[T_CTX:].reshape(DEC_BATCH, DEC_SEQ, D)
    return (y_prompt, y_sample, jnp.stack(ks, axis=1), jnp.stack(vs, axis=1), jnp.stack(sts, axis=1))
```

```python
import functools

import jax
import jax.numpy as jnp
from jax import lax
from jax.experimental import pallas as pl
from jax.experimental.pallas import tpu as pltpu

F32 = jnp.float32
BF16 = jnp.bfloat16

D = 1024
BATCH = 16
SEQ = 256
DEPTH = 4
DEC_BATCH = 4
DEC_SEQ = 1024
PAST_LEN = 512
GRID_W = 64
D_CONV = 256
D_SSM = 384
SSM_HEADS = 6
SSM_HEAD_DIM = 64
SSM_STATE = 128
CHUNK = 128
N_BC = 512
D_ATT = 384
ATT_HEADS = 6
ATT_HEAD_DIM = 64
NA_ROWS = 8
NA_COLS = 16
D_FF = 2816
N_EXPERTS = 8
D_FF_EXPERT = 1408
N_MOD = 6
EPS = 1e-6

T_CTX = BATCH * SEQ
T_LAT = DEC_BATCH * DEC_SEQ
T_ALL = T_CTX + T_LAT

W_A = 3 * D_CONV
W_S = 2 * D_SSM + N_BC
W_QKV = 3 * D_ATT
W_DT = 128
N_PROJ_PAD = W_A + W_S + W_QKV + W_DT

LANES = 128
NEG = -1e30

NA_TQ = 128
NA_TILES = DEC_SEQ // NA_TQ
NA_WIN_ROWS = 10
NA_WIN = NA_WIN_ROWS * GRID_W

VMEM_LIMIT = 56 * 1024 * 1024


def _cparams(n_axes):
    return pltpu.CompilerParams(dimension_semantics=("arbitrary",) * n_axes, vmem_limit_bytes=VMEM_LIMIT)


def _silu(x):
    return x * jax.nn.sigmoid(x)


def _mod_row(i, tm):
    n_ctx = T_CTX // tm
    per_lat = DEC_SEQ // tm
    return jnp.where(i < n_ctx, 0, (i - n_ctx) // per_lat + 1)


def _mod_spec(layer, k, tm):
    return pl.BlockSpec((None, None, 1, D), lambda i, *_: (layer, _mod_row(i, tm), 0, k))


ADA_TN = 1536


def _adaln_kernel(c_ref, w_ref, b_ref, o_ref):
    s = _silu(c_ref[...]).astype(BF16)
    o_ref[...] = jnp.dot(s, w_ref[...].astype(BF16), preferred_element_type=F32) + b_ref[...]


def _adaln_all(cvec, w_ada, b_ada):
    n = N_MOD * D
    return pl.pallas_call(
        _adaln_kernel,
        grid=(DEPTH, n // ADA_TN),
        in_specs=[
            pl.BlockSpec((8, D), lambda l, j: (0, 0)),
            pl.BlockSpec((None, D, ADA_TN), lambda l, j: (l, 0, j)),
            pl.BlockSpec((None, 1, ADA_TN), lambda l, j: (l, 0, j)),
        ],
        out_specs=pl.BlockSpec((None, 8, ADA_TN), lambda l, j: (l, 0, j)),
        out_shape=jax.ShapeDtypeStruct((DEPTH, 8, n), F32),
        compiler_params=_cparams(2),
        name="adaln",
    )(cvec, w_ada, b_ada.reshape(DEPTH, 1, n))


TM_IN = 512


def _norm_mod(x, nw, sh, sc):
    var = jnp.mean(x * x, axis=-1, keepdims=True)
    h = x * lax.rsqrt(var + EPS) * nw
    return h * (1.0 + sc) + sh


def _inproj_kernel(x_ref, nw_ref, sh_ref, sc_ref, w_ref, pa_ref, ps_ref, pqkv_ref, pdt_ref):
    hb = _norm_mod(x_ref[...], nw_ref[...], sh_ref[...], sc_ref[...]).astype(BF16)
    o = 0
    pa_ref[...] = jnp.dot(hb, w_ref[:, o:o + W_A], preferred_element_type=F32).astype(BF16)
    o += W_A
    ps_ref[...] = jnp.dot(hb, w_ref[:, o:o + W_S], preferred_element_type=F32).astype(BF16)
    o += W_S
    pqkv_ref[...] = jnp.dot(hb, w_ref[:, o:o + W_QKV], preferred_element_type=F32).astype(BF16)
    o += W_QKV
    pdt_ref[...] = jnp.dot(hb, w_ref[:, o:o + W_DT], preferred_element_type=F32)


def _inproj(x, mod4, norm1_w3, w_in_r, layer):
    tm = TM_IN
    row = lambda w: pl.BlockSpec((tm, w), lambda i: (i, 0))
    return pl.pallas_call(
        _inproj_kernel,
        grid=(T_ALL // tm,),
        in_specs=[
            row(D),
            pl.BlockSpec((None, 1, D), lambda i: (layer, 0, 0)),
            _mod_spec(layer, 0, tm),
            _mod_spec(layer, 1, tm),
            pl.BlockSpec((None, D, N_PROJ_PAD), lambda i: (layer, 0, 0)),
        ],
        out_specs=[row(W_A), row(W_S), row(W_QKV), row(W_DT)],
        out_shape=[
            jax.ShapeDtypeStruct((T_ALL, W_A), BF16),
            jax.ShapeDtypeStruct((T_ALL, W_S), BF16),
            jax.ShapeDtypeStruct((T_ALL, W_QKV), BF16),
            jax.ShapeDtypeStruct((T_ALL, W_DT), F32),
        ],
        compiler_params=_cparams(1),
        name="inproj",
    )(x, norm1_w3, mod4, mod4, w_in_r)


HALO = 16


def _conv3(cur, prev_row, next_row, w):
    n = cur.shape[0]
    rid = lax.broadcasted_iota(jnp.int32, cur.shape, 0)
    dn = jnp.where(rid == 0, prev_row, pltpu.roll(cur, 1, 0))
    up = jnp.where(rid == n - 1, next_row, pltpu.roll(cur, n - 1, 0))
    return w[0:1] * dn + w[1:2] * cur + w[2:3] * up


def _cumsum_rows(x):
    rid = lax.broadcasted_iota(jnp.int32, x.shape, 0)
    s = 1
    while s < x.shape[0]:
        x = x + jnp.where(rid >= s, pltpu.roll(x, s, 0), 0.0)
        s *= 2
    return x


def _spread_exact(x, onehot):
    hi = x.astype(BF16)
    rest = x - hi.astype(F32)
    mid = rest.astype(BF16)
    lo = (rest - mid.astype(F32)).astype(BF16)
    return (jnp.dot(hi, onehot, preferred_element_type=F32) + jnp.dot(mid, onehot, preferred_element_type=F32)
            + jnp.dot(lo, onehot, preferred_element_type=F32))


def _mixer_ab_kernel(seq_len, has_init, *refs):
    if has_init:
        (pa_ref, ps_ref, pdt_ref, init_ref, wa_ref, cw_ref, cb_ref, dtb_ref, alog_ref, dsk_ref, nw_ref, ctx_ref,
         both_ref, xs_ref, bc_ref, dt_ref, y_ref, rt_ref) = refs
        st_ref = None
        both_ref[0] = ctx_ref[...]
        yab_ref = both_ref.at[1]
    else:
        (pa_ref, ps_ref, pdt_ref, wa_ref, cw_ref, cb_ref, dtb_ref, alog_ref, dsk_ref, nw_ref,
         yab_ref, st_ref, xs_ref, bc_ref, dt_ref, y_ref, rt_ref) = refs
        init_ref = None
    nc = seq_len // CHUNK

    def halo(ref, r0, c):
        lo = jnp.maximum(r0 - HALO, 0)
        hi = jnp.minimum(r0 + CHUNK, seq_len - HALO)
        prev = ref[pl.ds(pl.multiple_of(lo, HALO), HALO), :].astype(F32)[HALO - 1:HALO]
        nxt = ref[pl.ds(pl.multiple_of(hi, HALO), HALO), :].astype(F32)[0:1]
        return jnp.where(c > 0, prev, 0.0), jnp.where(c < nc - 1, nxt, 0.0)

    def prep(c, carry):
        r0 = pl.multiple_of(c * CHUNK, CHUNK)
        rows = pl.ds(r0, CHUNK)
        pa = pa_ref[rows, :].astype(F32)
        pp, pn = halo(pa_ref, r0, c)
        g = pa[:, D_CONV:2 * D_CONV] * pa[:, 2 * D_CONV:]
        gp = pp[:, D_CONV:2 * D_CONV] * pp[:, 2 * D_CONV:]
        gn = pn[:, D_CONV:2 * D_CONV] * pn[:, 2 * D_CONV:]
        ya = pa[:, :D_CONV] * _conv3(g, gp, gn, wa_ref[...])
        yab_ref[rows, 0:D_CONV] = ya.astype(BF16)

        ps = ps_ref[rows, :].astype(F32)
        sp, sn = halo(ps_ref, r0, c)
        cw = cw_ref[...]
        cb = cb_ref[...]
        xs = _conv3(ps[:, :D_SSM], sp[:, :D_SSM], sn[:, :D_SSM], cw[:, :D_SSM]) + cb[:, :D_SSM]
        xs_ref[rows, :] = _silu(xs)
        o = 2 * D_SSM
        bc = _conv3(ps[:, o:], sp[:, o:], sn[:, o:], cw[:, D_SSM:]) + cb[:, D_SSM:]
        bc_ref[rows, :] = _silu(bc)
        t = pdt_ref[rows, :] + dtb_ref[...]
        dt_ref[rows, :] = jnp.maximum(t, 0.0) + jnp.log1p(jnp.exp(-jnp.abs(t)))
        return carry

    lax.fori_loop(0, nc, prep, 0)

    for d in range(2):
        for j in range(D_SSM // LANES):
            cols = slice(j * LANES, (j + 1) * LANES)
            if has_init:
                rt_ref[d, :, cols] = init_ref[d, cols, :].T
            else:
                rt_ref[d, :, cols] = jnp.zeros((SSM_STATE, LANES), F32)

    a_row = -jnp.exp(alog_ref[...])

    def ssd_chunk(c, d):
        rows = pl.ds(pl.multiple_of(c * CHUNK, CHUNK), CHUNK)
        xs = xs_ref[rows, :]
        bc = bc_ref[rows, :]
        dt = dt_ref[rows, :]
        adt = dt * a_row
        cs = _cumsum_rows(adt)
        total = cs[CHUNK - 1:CHUNK, :]
        e = cs if d == 0 else total - cs + adt
        e_t = e.T
        dec = jnp.exp(total - e)
        ee = jnp.exp(e)
        lane_k = lax.broadcasted_iota(jnp.int32, (LANES, D_SSM), 0)
        to_cols = jnp.where(lane_k == d * SSM_HEADS + lax.broadcasted_iota(jnp.int32, (LANES, D_SSM), 1)
                            // SSM_HEAD_DIM, 1.0, 0.0).astype(BF16)
        lane_k2 = lax.broadcasted_iota(jnp.int32, (LANES, SSM_HEADS * CHUNK), 0)
        to_blocks = jnp.where(lane_k2 == d * SSM_HEADS
                              + lax.broadcasted_iota(jnp.int32, (LANES, SSM_HEADS * CHUNK), 1) // CHUNK,
                              1.0, 0.0).astype(BF16)
        spread = _spread_exact(jnp.concatenate([dt, ee, dec], axis=0), to_cols)
        dt_x, ee_x, dec_x = spread[:CHUNK], spread[CHUNK:2 * CHUNK], spread[2 * CHUNK:]
        etot_x = ee_x[CHUNK - 1:CHUNK] if d == 0 else ee_x[0:1]
        e_blk = _spread_exact(e, to_blocks)
        xdt = xs * dt_x
        xdt_b = xdt.astype(BF16)
        xdd_b = (xdt * dec_x).astype(BF16)
        rt = rt_ref[d]
        rt_b = rt.astype(BF16)
        ri = lax.broadcasted_iota(jnp.int32, (CHUNK, CHUNK), 0)
        ci = lax.broadcasted_iota(jnp.int32, (CHUNK, CHUNK), 1)
        valid = (ri >= ci) if d == 0 else (ri <= ci)
        heads_per_group = SSM_HEADS // 2
        heads_per_tile = LANES // SSM_HEAD_DIM
        assert CHUNK == SSM_STATE == LANES
        tile_head = lax.broadcasted_iota(jnp.int32, (CHUNK, LANES), 1) // SSM_HEAD_DIM
        cgs, bg_ts, cbs = [], [], []
        for g in range(2):
            bg = bc[:, g * SSM_STATE:(g + 1) * SSM_STATE]
            cg = bc[:, 2 * SSM_STATE + g * SSM_STATE:2 * SSM_STATE + (g + 1) * SSM_STATE].astype(BF16)
            cgs.append(cg)
            bg_ts.append(bg.T.astype(BF16))
            cbs.append(lax.dot_general(cg, bg.astype(BF16), (((1,), (1,)), ((), ())), preferred_element_type=F32))
        y_tiles, st_tiles = [], []
        for j in range(D_SSM // LANES):
            tile = slice(j * LANES, (j + 1) * LANES)
            x_t, xdd_t, rt_t = xdt_b[:, tile], xdd_b[:, tile], rt_b[:, tile]
            y_t = jnp.zeros((CHUNK, LANES), F32)
            off_t = jnp.zeros((CHUNK, LANES), F32)
            st_t = jnp.zeros((SSM_STATE, LANES), F32)
            groups = sorted({(j * heads_per_tile + hh) // heads_per_group for hh in range(heads_per_tile)})
            for hh in range(heads_per_tile):
                h = j * heads_per_tile + hh
                g = h // heads_per_group
                k = d * SSM_HEADS + h
                mine = tile_head == hh
                seg = e_blk[:, h * CHUNK:(h + 1) * CHUNK] - e_t[k:k + 1, :]
                lm = jnp.where(valid, jnp.exp(jnp.where(valid, seg, 0.0)), 0.0)
                y_t = y_t + jnp.dot((cbs[g] * lm).astype(BF16), jnp.where(mine, x_t, jnp.zeros_like(x_t)),
                                    preferred_element_type=F32)
                if len(groups) > 1:
                    off_t = off_t + jnp.dot(cgs[g], jnp.where(mine, rt_t, jnp.zeros_like(rt_t)),
                                            preferred_element_type=F32)
                    st_t = jnp.where(mine, jnp.dot(bg_ts[g], xdd_t, preferred_element_type=F32), st_t)
            if len(groups) == 1:
                off_t = jnp.dot(cgs[groups[0]], rt_t, preferred_element_type=F32)
                st_t = jnp.dot(bg_ts[groups[0]], xdd_t, preferred_element_type=F32)
            y_tiles.append(y_t + off_t * ee_x[:, tile])
            st_tiles.append(st_t)
        y = jnp.concatenate(y_tiles, axis=1)
        if d == 0:
            y_ref[rows, :] = y
        else:
            y_ref[rows, :] = y_ref[rows, :] + y
        rt_ref[d] = rt * etot_x + jnp.concatenate(st_tiles, axis=1)

    lax.fori_loop(0, nc, lambda i, cr: (ssd_chunk(i, 0), cr)[1], 0)
    lax.fori_loop(0, nc, lambda i, cr: (ssd_chunk(nc - 1 - i, 1), cr)[1], 0)

    if not has_init:
        for d in range(2):
            for j in range(D_SSM // LANES):
                cols = slice(j * LANES, (j + 1) * LANES)
                st_ref[d, cols, :] = rt_ref[d, :, cols].T

    def finish(c, carry):
        rows = pl.ds(pl.multiple_of(c * CHUNK, CHUNK), CHUNK)
        y = y_ref[rows, :] + dsk_ref[...] * xs_ref[rows, :]
        y = y * _silu(ps_ref[rows, D_SSM:2 * D_SSM].astype(F32))
        col = lax.broadcasted_iota(jnp.int32, y.shape, 1)
        first = col < D_SSM // 2
        ysq = y * y
        s0 = jnp.sum(jnp.where(first, ysq, 0.0), axis=-1, keepdims=True)
        s1 = jnp.sum(jnp.where(first, 0.0, ysq), axis=-1, keepdims=True)
        inv = 1.0 / (D_SSM // 2)
        r = jnp.where(first, lax.rsqrt(s0 * inv + EPS), lax.rsqrt(s1 * inv + EPS))
        yab_ref[rows, D_CONV:] = (y * r * nw_ref[...]).astype(BF16)
        return carry

    lax.fori_loop(0, nc, finish, 0)


def _mixer_ab(pa, ps, pdt, init, params, seq_len, n_seq, blk0, layer, yab=None):
    has_init = init is not None
    n_w = D_CONV + D_SSM
    seq = lambda w: pl.BlockSpec((seq_len, w), lambda s: (blk0 + s, 0))
    full = lambda a: pl.BlockSpec(a.shape, lambda s: (0,) * a.ndim)
    in_specs = [seq(W_A), seq(W_S), seq(W_DT)]
    args = [pa, ps, pdt]
    if has_init:
        in_specs.append(pl.BlockSpec((None, None, 2, D_SSM, SSM_STATE), lambda s: (s, layer, 0, 0, 0)))
        args.append(init)
    in_specs += [full(p) for p in params]
    args += list(params)
    if has_init:
        assert T_CTX == T_LAT
        in_specs.append(pl.BlockSpec((seq_len, n_w), lambda s: (s, 0)))
        args.append(yab)
        out_specs = pl.BlockSpec((2, seq_len, n_w), lambda s: (0, s, 0))
        out_shape = jax.ShapeDtypeStruct((2, T_CTX, n_w), BF16)
    else:
        out_specs = [pl.BlockSpec((seq_len, n_w), lambda s: (s, 0)),
                     pl.BlockSpec((None, 2, D_SSM, SSM_STATE), lambda s: (s, 0, 0, 0))]
        out_shape = [jax.ShapeDtypeStruct((T_CTX, n_w), BF16),
                     jax.ShapeDtypeStruct((n_seq, 2, D_SSM, SSM_STATE), F32)]
    return pl.pallas_call(
        functools.partial(_mixer_ab_kernel, seq_len, has_init),
        grid=(n_seq,),
        in_specs=in_specs,
        out_specs=out_specs,
        out_shape=out_shape,
        scratch_shapes=[
            pltpu.VMEM((seq_len, D_SSM), F32),
            pltpu.VMEM((seq_len, N_BC), F32),
            pltpu.VMEM((seq_len, LANES), F32),
            pltpu.VMEM((seq_len, D_SSM), F32),
            pltpu.VMEM((2, SSM_STATE, D_SSM), F32),
        ],
        compiler_params=_cparams(1),
        name="mixer_ab_lat" if has_init else "mixer_ab_ctx",
    )(*args)


ATT_SCALE = ATT_HEAD_DIM ** -0.5


def _dot_nt(a, b):
    return lax.dot_general(a, b, (((1,), (1,)), ((), ())), preferred_element_type=F32)


def _attn_ctx_kernel(qkv_ref, o_ref):
    for h in range(ATT_HEADS):
        hc = slice(h * ATT_HEAD_DIM, (h + 1) * ATT_HEAD_DIM)
        q = qkv_ref[:, h * ATT_HEAD_DIM:(h + 1) * ATT_HEAD_DIM]
        k = qkv_ref[:, D_ATT + h * ATT_HEAD_DIM:D_ATT + (h + 1) * ATT_HEAD_DIM]
        v = qkv_ref[:, 2 * D_ATT + h * ATT_HEAD_DIM:2 * D_ATT + (h + 1) * ATT_HEAD_DIM]
        s = _dot_nt(q, k) * ATT_SCALE
        p = jnp.exp(s - jnp.max(s, axis=-1, keepdims=True))
        l = jnp.sum(p, axis=-1, keepdims=True)
        o = jnp.dot(p.astype(BF16), v, preferred_element_type=F32)
        o_ref[:, hc] = (o / l).astype(BF16)


def _attn_ctx(pqkv):
    return pl.pallas_call(
        _attn_ctx_kernel,
        grid=(BATCH,),
        in_specs=[pl.BlockSpec((SEQ, W_QKV), lambda s: (s, 0))],
        out_specs=pl.BlockSpec((SEQ, D_ATT), lambda s: (s, 0)),
        out_shape=jax.ShapeDtypeStruct((T_CTX, D_ATT), BF16),
        compiler_params=_cparams(1),
        name="attn_ctx",
    )(pqkv)


def _na_win_start(i):
    return jnp.clip(2 * i - NA_ROWS // 2, 0, DEC_SEQ // GRID_W - NA_WIN_ROWS)


NA_BIAS_FLAT = 1024


def _attn_na_kernel(qkv_ref, kc_ref, vc_ref, rpb_ref, ctx_ref, both_ref, bias_ref):
    both_ref[0] = ctx_ref[...]
    o_ref = both_ref.at[1]
    i = pl.program_id(1)
    q0 = pl.multiple_of(i * NA_TQ, NA_TQ)
    w_row = _na_win_start(i)
    w0 = pl.multiple_of(w_row * GRID_W, 2 * GRID_W)

    @pl.when(pl.program_id(0) == 0)
    def _():
        ql = lax.broadcasted_iota(jnp.int32, (NA_TQ, NA_WIN), 0) + q0
        kl = lax.broadcasted_iota(jnp.int32, (NA_TQ, NA_WIN), 1) + w0
        qr, qc = ql // GRID_W, ql % GRID_W
        kr, kc = kl // GRID_W, kl % GRID_W
        rs = jnp.clip(qr - NA_ROWS // 2, 0, DEC_SEQ // GRID_W - NA_ROWS)
        cs = jnp.clip(qc - NA_COLS // 2, 0, GRID_W - NA_COLS)
        valid = (kr >= rs) & (kr < rs + NA_ROWS) & (kc >= cs) & (kc < cs + NA_COLS)
        centre = (NA_ROWS - 1) * GRID_W + NA_COLS - 1
        shift = lax.rem(NA_BIAS_FLAT - (w0 - q0 + centre), NA_BIAS_FLAT)
        for h in range(ATT_HEADS):
            table = jnp.broadcast_to(rpb_ref[h], (NA_TQ, NA_BIAS_FLAT))
            rolled = pltpu.roll(table, shift, 1, stride=1, stride_axis=0)
            bias_ref[i, h] = jnp.where(valid, rolled[:, :NA_WIN], NEG)

    half = lambda n: lax.broadcasted_iota(jnp.int32, (n, LANES), 1) // ATT_HEAD_DIM
    half_q, half_loc, half_ctx = half(NA_TQ), half(NA_WIN), half(PAST_LEN)
    for j in range(ATT_HEADS * ATT_HEAD_DIM // LANES):
        t0 = j * LANES
        q_t = qkv_ref[pl.ds(q0, NA_TQ), t0:t0 + LANES]
        k_loc = qkv_ref[pl.ds(w0, NA_WIN), D_ATT + t0:D_ATT + t0 + LANES]
        v_loc = qkv_ref[pl.ds(w0, NA_WIN), 2 * D_ATT + t0:2 * D_ATT + t0 + LANES]
        k_ctx = kc_ref[:, t0:t0 + LANES].astype(BF16)
        v_ctx = vc_ref[:, t0:t0 + LANES].astype(BF16)
        out_t = jnp.zeros((NA_TQ, LANES), F32)
        for hh in range(LANES // ATT_HEAD_DIM):
            h = j * (LANES // ATT_HEAD_DIM) + hh
            q = jnp.where(half_q == hh, q_t, jnp.zeros_like(q_t))
            s_loc = _dot_nt(q, k_loc) * ATT_SCALE + bias_ref[i, h]
            s_ctx = _dot_nt(q, k_ctx) * ATT_SCALE
            m = jnp.maximum(jnp.max(s_loc, axis=-1, keepdims=True), jnp.max(s_ctx, axis=-1, keepdims=True))
            p_loc = jnp.exp(s_loc - m).astype(BF16)
            p_ctx = jnp.exp(s_ctx - m).astype(BF16)
            o = (jnp.dot(p_loc, jnp.where(half_loc == hh, v_loc, jnp.ones_like(v_loc)), preferred_element_type=F32)
                 + jnp.dot(p_ctx, jnp.where(half_ctx == hh, v_ctx, jnp.ones_like(v_ctx)),
                           preferred_element_type=F32))
            denom = pltpu.roll(o, ATT_HEAD_DIM, 1)
            out_t = jnp.where(half_q == hh, o / denom, out_t)
        o_ref[:, t0:t0 + LANES] = out_t.astype(BF16)


def _attn_na(pqkv, cache_k4, cache_v4, rpb_flat, yc_ctx, layer):
    assert T_CTX == T_LAT
    lat0 = T_CTX // DEC_SEQ
    return pl.pallas_call(
        _attn_na_kernel,
        grid=(DEC_BATCH, NA_TILES),
        in_specs=[
            pl.BlockSpec((DEC_SEQ, W_QKV), lambda b, i: (lat0 + b, 0)),
            pl.BlockSpec((None, None, PAST_LEN, D_ATT), lambda b, i: (b, layer, 0, 0)),
            pl.BlockSpec((None, None, PAST_LEN, D_ATT), lambda b, i: (b, layer, 0, 0)),
            pl.BlockSpec((None, ATT_HEADS, 1, NA_BIAS_FLAT), lambda b, i: (layer, 0, 0, 0)),
            pl.BlockSpec((NA_TQ, D_ATT), lambda b, i: (b * NA_TILES + i, 0)),
        ],
        out_specs=pl.BlockSpec((2, NA_TQ, D_ATT), lambda b, i: (0, b * NA_TILES + i, 0)),
        out_shape=jax.ShapeDtypeStruct((2, T_CTX, D_ATT), BF16),
        scratch_shapes=[pltpu.VMEM((NA_TILES, ATT_HEADS, NA_TQ, NA_WIN), F32)],
        compiler_params=_cparams(2),
        name="attn_na",
    )(pqkv, cache_k4, cache_v4, rpb_flat, yc_ctx)


TM_OUT = 512


R_E0, R_E1, R_W0, R_W1, R_P0, R_P1 = range(6)


def _outproj_kernel(with_router, *refs):
    if with_router:
        (yab_ref, yc_ref, x_ref, w_ref, g1_ref, nw_ref, sh_ref, sc_ref, rw_ref, rb_ref,
         x1_ref, h2_ref, route_ref, count_ref, base_ref) = refs
    else:
        yab_ref, yc_ref, x_ref, w_ref, g1_ref, nw_ref, sh_ref, sc_ref, x1_ref, h2_ref = refs
    n_ab = D_CONV + D_SSM
    out = (jnp.dot(yab_ref[...], w_ref[0:n_ab, :], preferred_element_type=F32)
           + jnp.dot(yc_ref[...], w_ref[n_ab:, :], preferred_element_type=F32))
    x1 = x_ref[...] + g1_ref[...] * out
    x1_ref[...] = x1
    h2 = _norm_mod(x1, nw_ref[...], sh_ref[...], sc_ref[...])
    h2_ref[...] = h2.astype(h2_ref.dtype)
    if with_router:
        @pl.when(pl.program_id(0) == 0)
        def _():
            base_ref[...] = jnp.zeros_like(base_ref)

        rw = rw_ref[...]
        h_hi, w_hi = h2.astype(BF16), rw.astype(BF16)
        h_lo, w_lo = (h2 - h_hi.astype(F32)).astype(BF16), (rw - w_hi.astype(F32)).astype(BF16)
        logits = (jnp.dot(h_hi, w_hi, preferred_element_type=F32) + jnp.dot(h_hi, w_lo, preferred_element_type=F32)
                  + jnp.dot(h_lo, w_hi, preferred_element_type=F32))
        logits = logits + rb_ref[...]
        lane = lax.broadcasted_iota(jnp.int32, logits.shape, 1)
        m1 = jnp.max(logits, axis=-1, keepdims=True)
        i1 = jnp.min(jnp.where(logits == m1, lane, LANES), axis=-1, keepdims=True)
        rest = jnp.where(lane == i1, -jnp.inf, logits)
        m2 = jnp.max(rest, axis=-1, keepdims=True)
        i2 = jnp.min(jnp.where(rest == m2, lane, LANES), axis=-1, keepdims=True)
        e2 = jnp.exp(m2 - m1)
        den = 1.0 + e2
        tm = logits.shape[0]
        sel0, sel1 = lane == i1, lane == i2
        picked = jnp.where(sel0 | sel1, 1.0, 0.0)
        earlier = (lax.broadcasted_iota(jnp.int32, (tm, tm), 1)
                   < lax.broadcasted_iota(jnp.int32, (tm, tm), 0)).astype(BF16)
        rank = base_ref[...] + jnp.dot(earlier, picked.astype(BF16), preferred_element_type=F32)
        p0 = jnp.sum(jnp.where(sel0, rank, 0.0), axis=-1, keepdims=True)
        p1 = jnp.sum(jnp.where(sel1, rank, 0.0), axis=-1, keepdims=True)
        base = base_ref[...] + jnp.sum(picked, axis=0, keepdims=True)
        base_ref[...] = base
        count_ref[...] = base
        rec = jnp.zeros(logits.shape, F32)
        for k, val in ((R_E0, i1.astype(F32)), (R_E1, i2.astype(F32)), (R_W0, 1.0 / den), (R_W1, e2 / den),
                       (R_P0, p0), (R_P1, p1)):
            rec = jnp.where(lane == k, val, rec)
        route_ref[...] = rec


def _outproj(yab, yc, x, mod4, w_out_b, norm2_w3, router, layer):
    tm = TM_OUT
    with_router = router is not None
    row = lambda w: pl.BlockSpec((tm, w), lambda i: (i, 0))
    in_specs = [
        row(D_CONV + D_SSM), row(D_ATT), row(D),
        pl.BlockSpec((None, D, D), lambda i: (layer, 0, 0)),
        _mod_spec(layer, 2, tm),
        pl.BlockSpec((None, 1, D), lambda i: (layer, 0, 0)),
        _mod_spec(layer, 3, tm),
        _mod_spec(layer, 4, tm),
    ]
    args = [yab, yc, x, w_out_b, mod4, norm2_w3, mod4, mod4]
    out_specs = [row(D), row(D)]
    out_shape = [jax.ShapeDtypeStruct((T_ALL, D), F32), jax.ShapeDtypeStruct((T_ALL, D), F32 if with_router else BF16)]
    scratch = []
    if with_router:
        rw, rb = router
        in_specs += [pl.BlockSpec(rw.shape, lambda i: (0, 0)), pl.BlockSpec(rb.shape, lambda i: (0, 0))]
        args += [rw, rb]
        out_specs += [row(LANES), pl.BlockSpec((1, LANES), lambda i: (0, 0))]
        out_shape += [jax.ShapeDtypeStruct((T_ALL, LANES), F32), jax.ShapeDtypeStruct((1, LANES), F32)]
        scratch = [pltpu.VMEM((1, LANES), F32)]
    return pl.pallas_call(
        functools.partial(_outproj_kernel, with_router),
        grid=(T_ALL // tm,),
        in_specs=in_specs,
        out_specs=out_specs,
        out_shape=out_shape,
        scratch_shapes=scratch,
        compiler_params=_cparams(1),
        name="outproj_router" if with_router else "outproj",
    )(*args)


TM_FF = 512


def _swiglu(h, w1, w3, w2):
    u = jnp.dot(h, w1, preferred_element_type=F32)
    v = jnp.dot(h, w3, preferred_element_type=F32)
    return jnp.dot((_silu(u) * v).astype(BF16), w2, preferred_element_type=F32)


def _ffn_kernel(h_ref, w1_ref, w3_ref, w2_ref, x_ref, g2_ref, o_ref, acc_ref):
    e = pl.program_id(1)

    @pl.when(e == 0)
    def _():
        acc_ref[...] = jnp.zeros_like(acc_ref)

    acc_ref[...] += _swiglu(h_ref[...], w1_ref[...], w3_ref[...], w2_ref[...])

    @pl.when(e == pl.num_programs(1) - 1)
    def _():
        o_ref[...] = x_ref[...] + g2_ref[...] * acc_ref[...]


def _ffn(h2, x1, mod4, w1, w3, w2, layer):
    tm = TM_FF
    n_chunks = w1.shape[0]
    row = lambda w: pl.BlockSpec((tm, w), lambda i, e: (i, 0))
    return pl.pallas_call(
        _ffn_kernel,
        grid=(T_ALL // tm, n_chunks),
        in_specs=[
            row(D),
            pl.BlockSpec((None, D, D_FF_EXPERT), lambda i, e: (e, 0, 0)),
            pl.BlockSpec((None, D, D_FF_EXPERT), lambda i, e: (e, 0, 0)),
            pl.BlockSpec((None, D_FF_EXPERT, D), lambda i, e: (e, 0, 0)),
            row(D),
            _mod_spec(layer, 5, tm),
        ],
        out_specs=row(D),
        out_shape=jax.ShapeDtypeStruct((T_ALL, D), F32),
        scratch_shapes=[pltpu.VMEM((tm, D), F32)],
        compiler_params=_cparams(2),
        name="ffn",
    )(h2, w1, w3, w2, x1, mod4)


TOP_K = 2
TM_X = 512
TM_R = 512
ROW_DMA_UNROLL = 8
N_SLOTS = TOP_K * T_ALL + N_EXPERTS * TM_X
N_XTILES = N_SLOTS // TM_X


def _route_plan(route, counts):
    cnt = counts[0, :N_EXPERTS].astype(jnp.int32)
    gsz = (cnt + TM_X - 1) // TM_X * TM_X
    ends = jnp.cumsum(gsz)
    offs = ends - gsz
    e = route[:, R_E0:R_E1 + 1].astype(jnp.int32)
    pos = route[:, R_P0:R_P1 + 1].astype(jnp.int32)
    sel = e[..., None] == jnp.arange(N_EXPERTS, dtype=jnp.int32)
    dest = jnp.sum(jnp.where(sel, offs, 0), axis=-1) + pos
    n_tiles = ends[-1] // TM_X
    tile = jnp.minimum(jnp.arange(N_XTILES, dtype=jnp.int32), n_tiles - 1)
    texp = jnp.sum((ends[None, :] <= (tile * TM_X)[:, None]).astype(jnp.int32), axis=-1)
    return (dest.reshape(T_ALL // TM_R, 1, TOP_K * TM_R).astype(jnp.int32), tile, texp.astype(jnp.int32),
            n_tiles.reshape(1).astype(jnp.int32))


def _row_copy(src, s, dst, d, sem):
    return pltpu.make_async_copy(src.at[pl.ds(s, 1)], dst.at[pl.ds(d, 1)], sem)


def _dispatch_kernel(dest_ref, h_ref, xs_in_ref, xs_ref, sem):
    del xs_in_ref

    def issue(t, carry):
        for k in range(TOP_K):
            _row_copy(h_ref, t, xs_ref, dest_ref[0, TOP_K * t + k], sem).start(priority=k)
        return carry

    lax.fori_loop(0, TM_R, issue, 0, unroll=ROW_DMA_UNROLL)

    def drain(t, carry):
        for k in range(TOP_K):
            _row_copy(h_ref, 0, xs_ref, 0, sem).wait()
        return carry

    lax.fori_loop(0, TM_R, drain, 0, unroll=ROW_DMA_UNROLL)


def _dispatch(dest, h2):
    xs0 = jnp.zeros((N_SLOTS, D), F32)
    return pl.pallas_call(
        _dispatch_kernel,
        grid=(T_ALL // TM_R,),
        in_specs=[
            pl.BlockSpec((None, 1, TOP_K * TM_R), lambda i: (i, 0, 0), memory_space=pltpu.SMEM),
            pl.BlockSpec((TM_R, D), lambda i: (i, 0)),
            pl.BlockSpec(memory_space=pl.ANY),
        ],
        out_specs=pl.BlockSpec(memory_space=pl.ANY),
        out_shape=jax.ShapeDtypeStruct((N_SLOTS, D), F32),
        scratch_shapes=[pltpu.SemaphoreType.DMA],
        input_output_aliases={2: 0},
        compiler_params=_cparams(1),
        name="moe_dispatch",
    )(dest, h2, xs0)


def _experts_kernel(tile_ref, texp_ref, nt_ref, xs_ref, w1_ref, w3_ref, w2_ref, ys_ref):
    del tile_ref, texp_ref
    live = pl.program_id(0) < nt_ref[0]

    @pl.when(live)
    def _():
        ys_ref[...] = _swiglu(xs_ref[...].astype(BF16), w1_ref[...], w3_ref[...], w2_ref[...])

    @pl.when(jnp.logical_not(live))
    def _():
        ys_ref[...] = jnp.zeros_like(ys_ref)


def _experts(tile, texp, n_tiles, xs, w1, w3, w2):
    wspec = lambda shape: pl.BlockSpec((None,) + shape, lambda j, tile, texp, nt: (texp[j], 0, 0))
    rows_in = pl.BlockSpec((TM_X, D), lambda j, tile, texp, nt: (tile[j], 0))
    rows_out = pl.BlockSpec((TM_X, D), lambda j, tile, texp, nt: (j, 0))
    return pl.pallas_call(
        _experts_kernel,
        grid_spec=pltpu.PrefetchScalarGridSpec(
            num_scalar_prefetch=3,
            grid=(N_XTILES,),
            in_specs=[rows_in, wspec((D, D_FF_EXPERT)), wspec((D, D_FF_EXPERT)), wspec((D_FF_EXPERT, D))],
            out_specs=rows_out,
        ),
        out_shape=jax.ShapeDtypeStruct((N_SLOTS, D), F32),
        compiler_params=_cparams(1),
        name="moe_experts",
    )(tile, texp, n_tiles, xs, w1, w3, w2)


def _combine_kernel(dest_ref, ys_ref, route_ref, x_ref, g2_ref, o_ref, y_ref, sem):
    def issue(t, carry):
        for k in range(TOP_K):
            _row_copy(ys_ref, dest_ref[0, TOP_K * t + k], y_ref.at[k], t, sem).start(priority=k)
        return carry

    lax.fori_loop(0, TM_R, issue, 0, unroll=ROW_DMA_UNROLL)

    def drain(t, carry):
        for k in range(TOP_K):
            _row_copy(ys_ref, 0, y_ref.at[k], 0, sem).wait()
        return carry

    lax.fori_loop(0, TM_R, drain, 0, unroll=ROW_DMA_UNROLL)
    route = route_ref[...]
    moe = route[:, R_W0:R_W0 + 1] * y_ref[0] + route[:, R_W1:R_W1 + 1] * y_ref[1]
    o_ref[...] = x_ref[...] + g2_ref[...] * moe


def _combine(dest, ys, route, x1, mod4, layer):
    row = lambda w: pl.BlockSpec((TM_R, w), lambda i: (i, 0))
    return pl.pallas_call(
        _combine_kernel,
        grid=(T_ALL // TM_R,),
        in_specs=[
            pl.BlockSpec((None, 1, TOP_K * TM_R), lambda i: (i, 0, 0), memory_space=pltpu.SMEM),
            pl.BlockSpec(memory_space=pl.ANY),
            row(LANES),
            row(D),
            _mod_spec(layer, 5, TM_R),
        ],
        out_specs=row(D),
        out_shape=jax.ShapeDtypeStruct((T_ALL, D), F32),
        scratch_shapes=[pltpu.VMEM((TOP_K, TM_R, D), F32), pltpu.SemaphoreType.DMA],
        compiler_params=_cparams(1),
        name="moe_combine",
    )(dest, ys, route, x1, mod4)


def _moe(h2, x1, route, counts, mod4, w1, w3, w2, layer):
    dest, tile, texp, n_tiles = _route_plan(route, counts)
    xs = _dispatch(dest, h2)
    ys = _experts(tile, texp, n_tiles, xs, w1, w3, w2)
    return _combine(dest, ys, route, x1, mod4, layer)


TM_NORM = 1024


def _final_norm_kernel(x_ref, w_ref, o_ref):
    x = x_ref[...]
    var = jnp.mean(x * x, axis=-1, keepdims=True)
    o_ref[...] = x * lax.rsqrt(var + EPS) * w_ref[...]


def _final_norm(x, w):
    return pl.pallas_call(
        _final_norm_kernel,
        grid=(T_ALL // TM_NORM,),
        in_specs=[pl.BlockSpec((TM_NORM, D), lambda i: (i, 0)), pl.BlockSpec((1, D), lambda i: (0, 0))],
        out_specs=pl.BlockSpec((TM_NORM, D), lambda i: (i, 0)),
        out_shape=jax.ShapeDtypeStruct((T_ALL, D), F32),
        compiler_params=_cparams(1),
        name="final_norm",
    )(x, w.reshape(1, D))


def _pad_lanes(a, fill=0.0):
    pad = [(0, 0)] * (a.ndim - 1) + [(0, LANES - a.shape[-1])]
    return jnp.pad(a, pad, constant_values=fill)


def _reorder_w_in(w_in):
    a_end = W_A
    s_end = a_end + W_S
    dt_end = s_end + 2 * SSM_HEADS
    w_dt = jnp.pad(w_in[:, :, s_end:dt_end], ((0, 0), (0, 0), (0, W_DT - 2 * SSM_HEADS)))
    return jnp.concatenate([w_in[:, :, :s_end], w_in[:, :, dt_end:], w_dt], axis=-1).astype(BF16)


def kernel(x_prompt, x_sample, cache_k, cache_v, state_ssm, c, c_ctx, norm1_w, norm2_w, w_ada, b_ada, w_in, w_out, conv_a_w, ssm_conv_w, ssm_conv_b, dt_bias, a_log, d_skip, ssm_norm_w, rpb, ffn_w1, ffn_w3, ffn_w2, router_w, router_b, moe_w1, moe_w3, moe_w2, final_norm_w):
    x = jnp.concatenate([x_prompt.reshape(T_CTX, D), x_sample.reshape(T_LAT, D)], axis=0)

    cvec = jnp.concatenate([c_ctx[None, :], c, jnp.zeros((8 - 1 - DEC_BATCH, D), F32)], axis=0)
    mod4 = _adaln_all(cvec, w_ada, b_ada).reshape(DEPTH, 8, 1, N_MOD * D)

    w_in_r = _reorder_w_in(w_in)
    w_out_b = w_out.astype(BF16)
    norm1_w3 = norm1_w.reshape(DEPTH, 1, D)
    norm2_w3 = norm2_w.reshape(DEPTH, 1, D)
    n_dense = ffn_w1.shape[0]
    split_cols = lambda w: jnp.transpose(w.reshape(n_dense, D, 2, D_FF_EXPERT), (0, 2, 1, 3)).astype(BF16)
    ffn_w1c, ffn_w3c = split_cols(ffn_w1), split_cols(ffn_w3)
    ffn_w2c = ffn_w2.reshape(n_dense, 2, D_FF_EXPERT, D).astype(BF16)
    moe_w1b, moe_w3b, moe_w2b = moe_w1.astype(BF16), moe_w3.astype(BF16), moe_w2.astype(BF16)
    router_wp = _pad_lanes(router_w)
    router_bp = _pad_lanes(router_b, NEG)
    dtb = _pad_lanes(dt_bias.reshape(DEPTH, 1, 2 * SSM_HEADS))
    alog = _pad_lanes(a_log.reshape(DEPTH, 1, 2 * SSM_HEADS))
    dsk = jnp.repeat(d_skip, SSM_HEAD_DIM, axis=-1).reshape(DEPTH, 1, D_SSM)
    rpb_flat = jnp.pad(rpb, ((0, 0), (0, 0), (0, 1), (0, GRID_W - rpb.shape[-1])))
    rpb_flat = rpb_flat.reshape(DEPTH, ATT_HEADS, 1, NA_BIAS_FLAT)
    cache_k4 = cache_k.reshape(DEC_BATCH, DEPTH, PAST_LEN, D_ATT)
    cache_v4 = cache_v.reshape(DEC_BATCH, DEPTH, PAST_LEN, D_ATT)
    state5 = state_ssm.reshape(DEC_BATCH, DEPTH, 2, D_SSM, SSM_STATE)

    ks, vs, sts = [], [], []
    for layer in range(DEPTH):
        pa, ps, pqkv, pdt = _inproj(x, mod4, norm1_w3, w_in_r, layer)
        params = (conv_a_w[layer], ssm_conv_w[layer], ssm_conv_b[layer].reshape(1, -1), dtb[layer], alog[layer],
                  dsk[layer], ssm_norm_w[layer].reshape(1, D_SSM))
        yab, st = _mixer_ab(pa, ps, pdt, None, params, SEQ, BATCH, 0, layer)
        yab = _mixer_ab(pa, ps, pdt, state5, params, DEC_SEQ, DEC_BATCH, T_CTX // DEC_SEQ, layer, yab)
        yab = yab.reshape(T_ALL, D_CONV + D_SSM)
        yc = _attn_na(pqkv, cache_k4, cache_v4, rpb_flat, _attn_ctx(pqkv), layer).reshape(T_ALL, D_ATT)
        ks.append(pqkv[:T_CTX, D_ATT:2 * D_ATT].astype(F32).reshape(BATCH, SEQ, ATT_HEADS, ATT_HEAD_DIM))
        vs.append(pqkv[:T_CTX, 2 * D_ATT:].astype(F32).reshape(BATCH, SEQ, ATT_HEADS, ATT_HEAD_DIM))
        sts.append(st.reshape(BATCH, 2, SSM_HEADS, SSM_HEAD_DIM, SSM_STATE))
        i = layer // 2
        if layer % 2 == 0:
            x1, h2 = _outproj(yab, yc, x, mod4, w_out_b, norm2_w3, None, layer)
            x = _ffn(h2, x1, mod4, ffn_w1c[i], ffn_w3c[i], ffn_w2c[i], layer)
        else:
            x1, h2, route, counts = _outproj(yab, yc, x, mod4, w_out_b, norm2_w3,
                                             (router_wp[i], router_bp[i][None, :]), layer)
            x = _moe(h2, x1, route, counts, mod4, moe_w1b[i], moe_w3b[i], moe_w2b[i], layer)

    y = _final_norm(x, final_norm_w)
    y_prompt = y[:T_CTX].reshape(BATCH, SEQ, D)
    y_sample = y[T_CTX:].reshape(DEC_BATCH, DEC_SEQ, D)
    return (y_prompt, y_sample, jnp.stack(ks, axis=1), jnp.stack(vs, axis=1), jnp.stack(sts, axis=1))
```
